```python
import jax, jax.numpy as jnp
from jax import lax
import numpy as np

D_MODEL = 1024
BATCH = 8
SEQ = 4096
DEPTH = 1

CHUNK = 64
GLA_HEADS = 4
GLA_DK = 64
GLA_DV = 128
GLA_GATE_RANK = 16
GLA_GATE_NORMALIZER = 16.0
GLA_NORM_EPS = 1e-5
RWKV_HEADS = 8
RWKV_HEAD = 64
RWKV_W_LORA = 64
RWKV_A_LORA = 64
RWKV_G_LORA = 128
RWKV_GN_EPS = 64e-5
L2_EPS = 1e-12
GLA_WIDTH = GLA_HEADS * GLA_DV
RWKV_WIDTH = RWKV_HEADS * RWKV_HEAD
BRANCH_WIDTH = 512
N_BRANCH = 2
D_FF = 4 * D_MODEL
LN_EPS = 1e-5
DEEPNORM_ALPHA = (2.0 * DEPTH) ** 0.25
DEEPNORM_BETA = (8.0 * DEPTH) ** -0.25
GLA_COLS = (GLA_HEADS * GLA_DK, GLA_HEADS * GLA_DK, GLA_WIDTH, GLA_WIDTH, GLA_GATE_RANK)
RWKV_COLS = (RWKV_WIDTH, RWKV_WIDTH, RWKV_WIDTH, RWKV_W_LORA, RWKV_A_LORA, RWKV_G_LORA)
GLA_IN = 2 * GLA_HEADS * GLA_DK + 2 * GLA_WIDTH + GLA_GATE_RANK
RWKV_IN = 3 * RWKV_WIDTH + RWKV_W_LORA + RWKV_A_LORA + RWKV_G_LORA
IN_WIDTH = GLA_IN + RWKV_IN

kernel_name = 'gla_rwkv7_gated_hybrid_deepnorm'


def _split(u, sizes):
    return jnp.split(u, np.cumsum(sizes)[:-1].tolist(), axis=-1)


def _layernorm(h, g, b, dtype):
    hf = h.astype(jnp.float32)
    mu = jnp.mean(hf, -1, keepdims=True)
    var = jnp.mean(jnp.square(hf - mu), -1, keepdims=True)
    return ((hf - mu) * lax.rsqrt(var + LN_EPS) * g + b).astype(dtype)


def _token_shift(u, mu):
    prev = jnp.pad(u, ((0, 0), (1, 0), (0, 0)))[:, :-1]
    return u + (prev - u) * mu


def _gla_chunk_step(state, inp):
    q, k, v, g = inp
    b = jnp.cumsum(g, axis=2)
    causal = jnp.tril(jnp.ones((CHUNK, CHUNK), dtype=bool))
    diff = b[:, :, :, None, :] - b[:, :, None, :, :]
    decay = jnp.exp(jnp.where(causal[None, None, :, :, None], diff, -jnp.inf))
    scores = jnp.einsum('bhid,bhjd,bhijd->bhij', q, k, decay)
    o = (jnp.einsum('bhij,bhjv->bhiv', scores, v)
         + jnp.einsum('bhid,bhdv->bhiv', q * jnp.exp(b), state))
    b_last = b[:, :, -1, :]
    state = (state * jnp.exp(b_last)[..., None]
             + jnp.einsum('bhjd,bhjv->bhdv', k * jnp.exp(b_last[:, :, None, :] - b), v))
    return state, o


def _gla_branch(h, w_gk_up, b_gk, gla_norm_w):
    Bsz, S, _ = h.shape
    q, k, v, g_out, gk_low = _split(h, GLA_COLS)
    gk = jax.nn.log_sigmoid((gk_low @ w_gk_up + b_gk).astype(jnp.float32)) / GLA_GATE_NORMALIZER
    n_chunks = S // CHUNK

    def to_chunks(t, d):
        return t.astype(jnp.float32).reshape(Bsz, n_chunks, CHUNK, GLA_HEADS, d).transpose(1, 0, 3, 2, 4)

    state0 = jnp.zeros((Bsz, GLA_HEADS, GLA_DK, GLA_DV), jnp.float32)
    _, o = lax.scan(_gla_chunk_step, state0,
                    (to_chunks(q * GLA_DK ** -0.5, GLA_DK), to_chunks(k, GLA_DK),
                     to_chunks(v, GLA_DV), to_chunks(gk, GLA_DK)))
    o = o.transpose(1, 0, 3, 2, 4).reshape(Bsz, S, GLA_HEADS, GLA_DV)
    o = o * lax.rsqrt(jnp.mean(jnp.square(o), -1, keepdims=True) + GLA_NORM_EPS) * gla_norm_w
    return o.reshape(Bsz, S, GLA_WIDTH) * jax.nn.silu(g_out.astype(jnp.float32))


def _rwkv7_step(state, inp):
    r, w, k, v, kk, kb = inp
    sa = -jnp.einsum('bhvk,bhk->bhv', state, kk)
    state = state * w[:, :, None, :] + sa[..., None] * kb[:, :, None, :] + v[..., None] * k[:, :, None, :]
    y = jnp.einsum('bhvk,bhk->bhv', state, r)
    return state, y


def _rwkv7_branch(h, mu_shift, w0, w_up, a0, a_up, g_up, k_k, k_a, r_k, gn_w, gn_b):
    Bsz, S, _ = h.shape
    u = _token_shift(h, mu_shift).astype(jnp.float32)
    r, k, v, w_low, a_low, g_low = _split(u, RWKV_COLS)
    w_log = -jax.nn.softplus(-(w0 + jnp.tanh(w_low) @ w_up)) - 0.5
    decay = jnp.exp(-jnp.exp(w_log))
    a = jax.nn.sigmoid(a0 + a_low @ a_up)
    g = jax.nn.sigmoid(g_low) @ g_up

    def heads(t):
        return t.reshape(Bsz, S, RWKV_HEADS, RWKV_HEAD)

    kk = heads(k * k_k)
    kk = kk / jnp.maximum(jnp.sqrt(jnp.sum(jnp.square(kk), -1, keepdims=True)), L2_EPS)
    k = k * (1.0 + (a - 1.0) * k_a)
    r_h, k_h, v_h, a_h, w_h = heads(r), heads(k), heads(v), heads(a), heads(decay)

    def tm(t):
        return jnp.swapaxes(t, 0, 1)

    state0 = jnp.zeros((Bsz, RWKV_HEADS, RWKV_HEAD, RWKV_HEAD), jnp.float32)
    _, y = lax.scan(_rwkv7_step, state0, (tm(r_h), tm(w_h), tm(k_h), tm(v_h), tm(kk), tm(kk * a_h)))
    y = tm(y)
    mu = jnp.mean(y, -1, keepdims=True)
    var = jnp.mean(jnp.square(y - mu), -1, keepdims=True)
    y = ((y - mu) * lax.rsqrt(var + RWKV_GN_EPS)).reshape(Bsz, S, RWKV_WIDTH) * gn_w + gn_b
    bonus = jnp.sum(r_h * k_h * r_k, -1, keepdims=True) * v_h
    return (y + bonus.reshape(Bsz, S, RWKV_WIDTH)) * g


def setup_inputs(seed: int = 0) -> dict:
    key = jax.random.key(seed)
    ks = iter(jax.random.split(key, 32))
    L = DEPTH

    def nrm(shape, scale):
        return scale * jax.random.normal(next(ks), shape, jnp.float32)

    col_scale = np.ones((IN_WIDTH,), np.float32)
    g_off = np.cumsum((0,) + GLA_COLS)
    r_off = GLA_IN + np.cumsum((0,) + RWKV_COLS)
    col_scale[g_off[2]:g_off[3]] = DEEPNORM_BETA
    col_scale[r_off[2]:r_off[3]] = DEEPNORM_BETA
    n = np.arange(RWKV_WIDTH, dtype=np.float32)
    decay_speed = (-7.0 + 5.0 * (n / (RWKV_WIDTH - 1)) ** 0.85 + 0.5).astype(np.float32)

    x = nrm((BATCH, SEQ, D_MODEL), 1.0)
    w_in = nrm((L, D_MODEL, IN_WIDTH), D_MODEL ** -0.5) * jnp.asarray(col_scale)
    mu_shift = jax.random.uniform(next(ks), (L, RWKV_IN), jnp.float32)
    w_gk_up = nrm((L, GLA_GATE_RANK, GLA_HEADS * GLA_DK), GLA_GATE_RANK ** -0.5)
    b_gk = nrm((L, GLA_HEADS * GLA_DK), 0.1)
    gla_norm_w = 1.0 + nrm((L, GLA_DV), 0.02)
    rwkv_w0 = jnp.asarray(decay_speed) + nrm((L, RWKV_WIDTH), 0.1)
    rwkv_w_up = nrm((L, RWKV_W_LORA, RWKV_WIDTH), 0.1 * RWKV_W_LORA ** -0.5)
    rwkv_a0 = nrm((L, RWKV_WIDTH), 0.1)
    rwkv_a_up = nrm((L, RWKV_A_LORA, RWKV_WIDTH), 0.5 * RWKV_A_LORA ** -0.5)
    rwkv_g_up = nrm((L, RWKV_G_LORA, RWKV_WIDTH), RWKV_G_LORA ** -0.5)
    rwkv_k_k = 0.85 + nrm((L, RWKV_WIDTH), 0.02)
    rwkv_k_a = 1.0 + nrm((L, RWKV_WIDTH), 0.02)
    rwkv_r_k = -0.04 + nrm((L, RWKV_HEADS, RWKV_HEAD), 0.02)
    rwkv_gn_w = 1.0 + nrm((L, RWKV_WIDTH), 0.02)
    rwkv_gn_b = nrm((L, RWKV_WIDTH), 0.02)
    w_merge = nrm((L, D_MODEL, N_BRANCH * D_MODEL), D_MODEL ** -0.5)
    b_merge = nrm((L, N_BRANCH * D_MODEL), 0.02)
    w_branch = nrm((L, N_BRANCH, BRANCH_WIDTH, D_MODEL), DEEPNORM_BETA * BRANCH_WIDTH ** -0.5)
    w_out = nrm((L, D_MODEL, D_MODEL), DEEPNORM_BETA * D_MODEL ** -0.5)
    ln1_g = 1.0 + nrm((L, D_MODEL), 0.02)
    ln1_b = nrm((L, D_MODEL), 0.02)
    w_mlp_up = nrm((L, D_MODEL, D_FF), DEEPNORM_BETA * D_MODEL ** -0.5)
    b_mlp_up = nrm((L, D_FF), 0.02)
    w_mlp_down = nrm((L, D_FF, D_MODEL), DEEPNORM_BETA * D_FF ** -0.5)
    b_mlp_down = nrm((L, D_MODEL), 0.02)
    ln2_g = 1.0 + nrm((L, D_MODEL), 0.02)
    ln2_b = nrm((L, D_MODEL), 0.02)
    return {'x': x, 'w_in': w_in, 'mu_shift': mu_shift, 'w_gk_up': w_gk_up, 'b_gk': b_gk,
            'gla_norm_w': gla_norm_w, 'rwkv_w0': rwkv_w0, 'rwkv_w_up': rwkv_w_up,
            'rwkv_a0': rwkv_a0, 'rwkv_a_up': rwkv_a_up, 'rwkv_g_up': rwkv_g_up,
            'rwkv_k_k': rwkv_k_k, 'rwkv_k_a': rwkv_k_a, 'rwkv_r_k': rwkv_r_k,
            'rwkv_gn_w': rwkv_gn_w, 'rwkv_gn_b': rwkv_gn_b, 'w_merge': w_merge, 'b_merge': b_merge,
            'w_branch': w_branch, 'w_out': w_out, 'ln1_g': ln1_g, 'ln1_b': ln1_b,
            'w_mlp_up': w_mlp_up, 'b_mlp_up': b_mlp_up, 'w_mlp_down': w_mlp_down,
            'b_mlp_down': b_mlp_down, 'ln2_g': ln2_g, 'ln2_b': ln2_b}


def reference(x, w_in, mu_shift, w_gk_up, b_gk, gla_norm_w, rwkv_w0, rwkv_w_up, rwkv_a0,
              rwkv_a_up, rwkv_g_up, rwkv_k_k, rwkv_k_a, rwkv_r_k, rwkv_gn_w, rwkv_gn_b,
              w_merge, b_merge, w_branch, w_out, ln1_g, ln1_b, w_mlp_up, b_mlp_up,
              w_mlp_down, b_mlp_down, ln2_g, ln2_b):
    dt = x.dtype
    Bsz, S, _ = x.shape
    for l in range(DEPTH):
        h = x @ w_in[l]
        o_a = _gla_branch(h[..., :GLA_IN], w_gk_up[l], b_gk[l], gla_norm_w[l])
        o_b = _rwkv7_branch(h[..., GLA_IN:], mu_shift[l], rwkv_w0[l], rwkv_w_up[l], rwkv_a0[l],
                            rwkv_a_up[l], rwkv_g_up[l], rwkv_k_k[l], rwkv_k_a[l], rwkv_r_k[l],
                            rwkv_gn_w[l], rwkv_gn_b[l])
        branches = jnp.stack([o_a, o_b], axis=2).astype(dt)
        y = jnp.einsum('bsnc,ncd->bsnd', branches, w_branch[l])
        gates = jax.nn.sigmoid((x @ w_merge[l] + b_merge[l]).astype(jnp.float32))
        gates = gates.reshape(Bsz, S, N_BRANCH, D_MODEL)
        mix = jnp.sum(gates * y, axis=2).astype(dt) @ w_out[l]
        x = _layernorm(DEEPNORM_ALPHA * x + mix, ln1_g[l], ln1_b[l], dt)
        ffn = jnp.square(jax.nn.relu(x @ w_mlp_up[l] + b_mlp_up[l])) @ w_mlp_down[l] + b_mlp_down[l]
        x = _layernorm(DEEPNORM_ALPHA * x + ffn, ln2_g[l], ln2_b[l], dt)
    return x
```

```python
import functools

import jax
import jax.numpy as jnp
import numpy as np
from jax import lax
from jax.experimental import pallas as pl
from jax.experimental.pallas import tpu as pltpu

F32 = jnp.float32
BF16 = jnp.bfloat16

D_MODEL = 1024
D_FF = 4 * D_MODEL
DEPTH = 1
CHUNK = 64
SUB = 16
GLA_HEADS, GLA_DK, GLA_DV = 4, 64, 128
GLA_GATE_RANK = 16
GLA_GATE_NORMALIZER = 16.0
GLA_NORM_EPS = 1e-5
RWKV_HEADS, RWKV_HEAD = 8, 64
RWKV_GN_EPS = 64e-5
L2_EPS = 1e-12
LN_EPS = 1e-5
ALPHA = (2.0 * DEPTH) ** 0.25
GLA_QK = GLA_HEADS * GLA_DK
GLA_WIDTH = GLA_HEADS * GLA_DV
RWKV_WIDTH = RWKV_HEADS * RWKV_HEAD
GLA_IN = 2 * GLA_QK + 2 * GLA_WIDTH + GLA_GATE_RANK
GLA_IN_PAD = 1664
RWKV_IN = 3 * RWKV_WIDTH + 64 + 64 + 128
LORA_OFF = 3 * RWKV_WIDTH
LORA_IN = 256
FF_CHUNK = 1024
VMEM_LIMIT_V7X = 56 * 1024 * 1024


def _dot(a, b):
    return jnp.dot(a, b, preferred_element_type=F32)


def _dot_nt(a, b):
    return lax.dot_general(a, b, (((1,), (1,)), ((), ())), preferred_element_type=F32)


def _dot_tn(a, b):
    return lax.dot_general(a, b, (((0,), (0,)), ((), ())), preferred_element_type=F32)


def _sigmoid(z):
    return 1.0 / (1.0 + jnp.exp(-z))


def _log_sigmoid(z):
    return jnp.minimum(z, 0.0) - jnp.log(1.0 + jnp.exp(-jnp.abs(z)))


def _split_bf16(a):
    hi = a.astype(BF16)
    lo = (a - hi.astype(F32)).astype(BF16)
    return hi, lo


def _const_spec(shape):
    nd = len(shape)
    return pl.BlockSpec(shape, lambda *_: (0,) * nd, pipeline_mode=pl.Buffered(1))


def _proj_kernel(x_ref, wg_ref, wr_ref, mu_ref, wgk_ref, bgk_ref, wlora_ref, w0_ref, a0_ref,
                 kk_ref, ka_ref, rk_ref, ones_ref,
                 q_ref, k_ref, v_ref, sg_ref, gk_ref,
                 r_ref, rk2_ref, rv_ref, kkn_ref, kb_ref, lw_ref, g_ref, bonus_ref,
                 carry_ref, *, tiles_per_seq):
    i = pl.program_id(0)
    tm = x_ref.shape[0]
    xb = x_ref[...].astype(BF16)

    hg = _dot(xb, wg_ref[...])
    q_ref[...] = (hg[:, 0:GLA_QK] * (GLA_DK ** -0.5)).astype(BF16)
    k_ref[...] = hg[:, GLA_QK:2 * GLA_QK].astype(BF16)
    v_ref[...] = hg[:, 2 * GLA_QK:2 * GLA_QK + GLA_WIDTH].astype(BF16)
    g_out = hg[:, 2 * GLA_QK + GLA_WIDTH:2 * GLA_QK + 2 * GLA_WIDTH]
    sg_ref[...] = (g_out * _sigmoid(g_out)).astype(BF16)
    z = _dot(hg[:, 1536:1664].astype(BF16), wgk_ref[...]) + bgk_ref[...]
    gk_ref[...] = _log_sigmoid(z) * (1.0 / GLA_GATE_NORMALIZER)

    hr = _dot(xb, wr_ref[...])

    @pl.when(i % tiles_per_seq == 0)
    def _():
        carry_ref[...] = jnp.zeros_like(carry_ref)

    row = lax.broadcasted_iota(jnp.int32, (tm, 1), 0)
    prev = jnp.where(row == 0, carry_ref[...], pltpu.roll(hr, 1, 0))
    carry_ref[...] = hr[tm - 1:tm, :]
    u = hr + (prev - hr) * mu_ref[...]

    r = u[:, 0:RWKV_WIDTH]
    k = u[:, RWKV_WIDTH:2 * RWKV_WIDTH]
    v = u[:, 2 * RWKV_WIDTH:3 * RWKV_WIDTH]
    low = u[:, LORA_OFF:LORA_OFF + LORA_IN]
    col = lax.broadcasted_iota(jnp.int32, (1, LORA_IN), 1)
    low = jnp.where(col < 64, jnp.tanh(low), jnp.where(col < 128, low, _sigmoid(low)))
    lo = _dot(low.astype(BF16), wlora_ref[...])
    lw_ref[...] = -float(np.exp(-0.5)) * _sigmoid(w0_ref[...] + lo[:, 0:RWKV_WIDTH])
    a = _sigmoid(a0_ref[...] + lo[:, RWKV_WIDTH:2 * RWKV_WIDTH])
    g_ref[...] = lo[:, 2 * RWKV_WIDTH:3 * RWKV_WIDTH].astype(BF16)

    ones = ones_ref[...]
    kk = k * kk_ref[...]
    ss = _dot((kk * kk).astype(BF16), ones)
    kkn = kk / jnp.maximum(jnp.sqrt(ss), L2_EPS)
    k2 = k * (1.0 + (a - 1.0) * ka_ref[...])
    bonus = _dot((r * k2 * rk_ref[...]).astype(BF16), ones) * v
    r_ref[...] = r.astype(BF16)
    rk2_ref[...] = k2.astype(BF16)
    rv_ref[...] = v.astype(BF16)
    kkn_ref[...] = kkn.astype(BF16)
    kb_ref[...] = (kkn * a).astype(BF16)
    bonus_ref[...] = bonus.astype(BF16)


def _gla_kernel(q_ref, k_ref, v_ref, sg_ref, gk_ref, nw_ref, tri_ref, e_ref, o_ref, s_ref, *, nc):
    i = pl.program_id(0)

    @pl.when(i % nc == 0)
    def _():
        s_ref[...] = jnp.zeros_like(s_ref)

    tri = tri_ref[...]
    g_hi, g_lo = _split_bf16(gk_ref[...])
    b = _dot(tri, g_hi) + _dot(tri, g_lo)
    q = q_ref[...].astype(F32)
    k = k_ref[...].astype(F32)
    v = v_ref[...].astype(F32)
    row = lax.broadcasted_iota(jnp.int32, (CHUNK, 1), 0)
    rmod = row % SUB

    e = e_ref[...]
    diag = jnp.zeros((CHUNK, GLA_WIDTH), F32)
    for d in range(SUB):
        ks = k if d == 0 else pltpu.roll(k, d, 0)
        bs = b if d == 0 else pltpu.roll(b, d, 0)
        vs = v if d == 0 else pltpu.roll(v, d, 0)
        xd = q * ks * jnp.exp(jnp.minimum(b - bs, 0.0))
        xd = jnp.where(rmod >= d, xd, 0.0)
        diag = diag + _dot(xd.astype(BF16), e) * vs

    b_last = b[CHUNK - 1:CHUNK, :]
    qe = (q * jnp.exp(b)).astype(BF16)
    kdec = (k * jnp.exp(b_last - b)).astype(BF16)
    s_all = s_ref[...]
    outs = []
    for h in range(GLA_HEADS):
        sl = slice(h * GLA_DK, (h + 1) * GLA_DK)
        vl = slice(h * GLA_DV, (h + 1) * GLA_DV)
        qh, kh, bh = q[:, sl], k[:, sl], b[:, sl]
        vh = v[:, vl].astype(BF16)
        rows = [jnp.zeros((SUB, CHUNK), F32)]
        for blk in range(1, CHUNK // SUB):
            lo_r = blk * SUB
            ref = bh[lo_r - 1:lo_r, :]
            qi = qh[lo_r:lo_r + SUB, :] * jnp.exp(bh[lo_r:lo_r + SUB, :] - ref)
            ki = jnp.where(row < lo_r, kh * jnp.exp(jnp.minimum(ref - bh, 0.0)), 0.0)
            rows.append(_dot_nt(qi.astype(BF16), ki.astype(BF16)))
        sc = jnp.concatenate(rows, axis=0)
        sh = s_all[:, sl]
        oh = _dot_nt(qe[:, sl], sh.astype(BF16)) + _dot(sc.astype(BF16), vh) + diag[:, vl]
        ms = jnp.mean(oh * oh, axis=-1, keepdims=True)
        outs.append(oh * lax.rsqrt(ms + GLA_NORM_EPS))
        s_ref[:, sl] = sh * jnp.exp(b_last[:, sl]) + _dot_tn(vh, kdec[:, sl])
    o = jnp.concatenate(outs, axis=-1) * nw_ref[...]
    o_ref[...] = (o * sg_ref[...].astype(F32)).astype(BF16)


def _rwkv_kernel(r_ref, k_ref, v_ref, kk_ref, kb_ref, lw_ref, g_ref, bonus_ref, gnw_ref, gnb_ref,
                 tri_ref, o_ref, s_ref, *, nc):
    i = pl.program_id(0)

    @pl.when(i % nc == 0)
    def _():
        s_ref[...] = jnp.zeros_like(s_ref)

    tri = tri_ref[...]
    lw = lw_ref[...]
    l_hi, l_lo = _split_bf16(lw)
    c = _dot(tri, l_hi) + _dot(tri, l_lo)
    c_last = c[CHUNK - 1:CHUNK, :]
    e_c = jnp.exp(c)
    e_nc = jnp.exp(-c)
    e_rem = jnp.exp(c_last - c)
    r = r_ref[...].astype(F32)
    k = k_ref[...].astype(F32)
    kk = kk_ref[...].astype(F32)
    kb = kb_ref[...].astype(F32)
    v_all = v_ref[...]
    at = (-kk * jnp.exp(c - lw)).astype(BF16)
    rt = (r * e_c).astype(BF16)
    bt = (kb * e_nc).astype(BF16)
    kt = (k * e_nc).astype(BF16)
    bh_all = (kb * e_rem).astype(BF16)
    kh_all = (k * e_rem).astype(BF16)
    e_last = jnp.exp(c_last)

    row = lax.broadcasted_iota(jnp.int32, (CHUNK, CHUNK), 0)
    col = lax.broadcasted_iota(jnp.int32, (CHUNK, CHUNK), 1)
    strict = row > col
    incl = row >= col
    eye = (row == col).astype(F32)
    for h in range(RWKV_HEADS):
        sl = slice(h * RWKV_HEAD, (h + 1) * RWKV_HEAD)
        vh = v_all[:, sl]
        a_mat = _dot_nt(jnp.concatenate([at[:, sl], rt[:, sl]], axis=0),
                        jnp.concatenate([bt[:, sl], kt[:, sl]], axis=0))
        a_ab = jnp.where(strict, a_mat[0:CHUNK, 0:CHUNK], 0.0)
        a_ak = jnp.where(strict, a_mat[0:CHUNK, CHUNK:], 0.0)
        a_rb = jnp.where(incl, a_mat[CHUNK:, 0:CHUNK], 0.0)
        a_rk = jnp.where(incl, a_mat[CHUNK:, CHUNK:], 0.0)
        t_inv = eye + a_ab
        p = a_ab
        for _ in range(5):
            pb = p.astype(BF16)
            p = _dot(pb, pb)
            t_inv = t_inv + _dot(t_inv.astype(BF16), p.astype(BF16))
        tb = t_inv.astype(BF16)
        w_til = _dot(tb, at[:, sl])
        u_til = _dot(tb, _dot(a_ak.astype(BF16), vh).astype(BF16))
        sh = s_ref[:, sl]
        shb = sh.astype(BF16)
        u = _dot_nt(w_til.astype(BF16), shb) + u_til
        ub = u.astype(BF16)
        y = _dot_nt(rt[:, sl], shb) + _dot(a_rb.astype(BF16), ub) + _dot(a_rk.astype(BF16), vh)
        s_ref[:, sl] = sh * e_last[:, sl] + _dot_tn(
            jnp.concatenate([ub, vh], axis=0),
            jnp.concatenate([bh_all[:, sl], kh_all[:, sl]], axis=0))
        mu = jnp.mean(y, axis=-1, keepdims=True)
        yc = y - mu
        var = jnp.mean(yc * yc, axis=-1, keepdims=True)
        yn = yc * lax.rsqrt(var + RWKV_GN_EPS) * gnw_ref[:, sl] + gnb_ref[:, sl]
        o_ref[:, sl] = ((yn + bonus_ref[:, sl].astype(F32)) * g_ref[:, sl].astype(F32)).astype(BF16)


def _layernorm(h, g, b):
    mu = jnp.mean(h, axis=-1, keepdims=True)
    hc = h - mu
    var = jnp.mean(hc * hc, axis=-1, keepdims=True)
    return hc * lax.rsqrt(var + LN_EPS) * g + b


def _mix_kernel(x_ref, oa_ref, ob_ref, wm_ref, bm_ref, wba_ref, wbb_ref, wout_ref, ln1g_ref, ln1b_ref,
                w1_ref, b1_ref, w2_ref, b2_ref, ln2g_ref, ln2b_ref, out_ref):
    x = x_ref[...]
    xb = x.astype(BF16)
    ya = _dot(oa_ref[...], wba_ref[...])
    yb = _dot(ob_ref[...], wbb_ref[...])
    ga = _sigmoid(_dot(xb, wm_ref[:, 0:D_MODEL]) + bm_ref[:, 0:D_MODEL])
    m = ga * ya
    gb = _sigmoid(_dot(xb, wm_ref[:, D_MODEL:]) + bm_ref[:, D_MODEL:])
    m = m + gb * yb
    mix = _dot(m.astype(BF16), wout_ref[...])
    x1 = _layernorm(ALPHA * x + mix, ln1g_ref[...], ln1b_ref[...])
    x1b = x1.astype(BF16)
    acc = jnp.zeros_like(x1)
    for c in range(D_FF // FF_CHUNK):
        cs = slice(c * FF_CHUNK, (c + 1) * FF_CHUNK)
        h = jnp.maximum(_dot(x1b, w1_ref[:, cs]) + b1_ref[:, cs], 0.0)
        acc = acc + _dot((h * h).astype(BF16), w2_ref[cs, :])
    out_ref[...] = _layernorm(ALPHA * x1 + acc + b2_ref[...], ln2g_ref[...], ln2b_ref[...])


def _block_ones(n, blk):
    idx = np.arange(n) // blk
    return jnp.asarray(idx[:, None] == idx[None, :], BF16)


def _layer(x2, bsz, seq, p, tm_proj, tm_mix):
    t = bsz * seq
    row = lambda a: a.reshape(1, -1).astype(F32)

    w_in = p['w_in']
    wg = jnp.pad(w_in[:, :GLA_IN], ((0, 0), (0, GLA_IN_PAD - GLA_IN))).astype(BF16)
    wr = w_in[:, GLA_IN:].astype(BF16)
    wgk = jnp.pad(p['w_gk_up'], ((0, 128 - GLA_GATE_RANK), (0, 0))).astype(BF16)
    wlora = jnp.zeros((LORA_IN, 3 * RWKV_WIDTH), F32)
    wlora = wlora.at[0:64, 0:RWKV_WIDTH].set(p['rwkv_w_up'])
    wlora = wlora.at[64:128, RWKV_WIDTH:2 * RWKV_WIDTH].set(p['rwkv_a_up'])
    wlora = wlora.at[128:256, 2 * RWKV_WIDTH:].set(p['rwkv_g_up']).astype(BF16)
    ones_r = _block_ones(RWKV_WIDTH, RWKV_HEAD)

    n_tiles = t // tm_proj
    tok = lambda w: pl.BlockSpec((tm_proj, w), lambda i: (i, 0))
    bshape = lambda w: jax.ShapeDtypeStruct((t, w), BF16)
    proj_out = pl.pallas_call(
        functools.partial(_proj_kernel, tiles_per_seq=seq // tm_proj),
        grid=(n_tiles,),
        in_specs=[tok(D_MODEL), _const_spec((D_MODEL, GLA_IN_PAD)), _const_spec((D_MODEL, RWKV_IN)),
                  _const_spec((1, RWKV_IN)), _const_spec((128, GLA_QK)), _const_spec((1, GLA_QK)),
                  _const_spec((LORA_IN, 3 * RWKV_WIDTH))] + [_const_spec((1, RWKV_WIDTH))] * 5
                 + [_const_spec((RWKV_WIDTH, RWKV_WIDTH))],
        out_specs=[tok(GLA_QK), tok(GLA_QK), tok(GLA_WIDTH), tok(GLA_WIDTH), tok(GLA_QK)]
                  + [tok(RWKV_WIDTH)] * 8,
        out_shape=[bshape(GLA_QK), bshape(GLA_QK), bshape(GLA_WIDTH), bshape(GLA_WIDTH),
                   jax.ShapeDtypeStruct((t, GLA_QK), F32)]
                  + [bshape(RWKV_WIDTH)] * 5 + [jax.ShapeDtypeStruct((t, RWKV_WIDTH), F32)]
                  + [bshape(RWKV_WIDTH)] * 2,
        scratch_shapes=[pltpu.VMEM((1, RWKV_IN), F32)],
        compiler_params=pltpu.CompilerParams(dimension_semantics=("arbitrary",),
                                             vmem_limit_bytes=VMEM_LIMIT_V7X),
        name="proj_prep",
    )(x2, wg, wr, row(p['mu_shift']), wgk, row(p['b_gk']), wlora, row(p['rwkv_w0']), row(p['rwkv_a0']),
      row(p['rwkv_k_k']), row(p['rwkv_k_a']), row(p['rwkv_r_k']), ones_r)
    gq, gk_, gv, gsg, ggk, rr, rk2, rv, rkk, rkb, rlw, rg, rbonus = proj_out

    nc = seq // CHUNK
    n_chunks = t // CHUNK
    ctok = lambda w: pl.BlockSpec((CHUNK, w), lambda i: (i, 0))
    tri = jnp.asarray(np.tril(np.ones((CHUNK, CHUNK), np.float32)), BF16)
    hid = np.arange(GLA_QK) // GLA_DK
    vid = np.arange(GLA_WIDTH) // GLA_DV
    e_mat = jnp.asarray(hid[:, None] == vid[None, :], BF16)
    o_a = pl.pallas_call(
        functools.partial(_gla_kernel, nc=nc),
        grid=(n_chunks,),
        in_specs=[ctok(GLA_QK), ctok(GLA_QK), ctok(GLA_WIDTH), ctok(GLA_WIDTH), ctok(GLA_QK),
                  _const_spec((1, GLA_WIDTH)), _const_spec((CHUNK, CHUNK)), _const_spec((GLA_QK, GLA_WIDTH))],
        out_specs=ctok(GLA_WIDTH),
        out_shape=bshape(GLA_WIDTH),
        scratch_shapes=[pltpu.VMEM((GLA_DV, GLA_QK), F32)],
        compiler_params=pltpu.CompilerParams(dimension_semantics=("arbitrary",),
                                             vmem_limit_bytes=VMEM_LIMIT_V7X),
        name="gla_chunk",
    )(gq, gk_, gv, gsg, ggk, row(jnp.tile(p['gla_norm_w'], GLA_HEADS)), tri, e_mat)

    o_b = pl.pallas_call(
        functools.partial(_rwkv_kernel, nc=nc),
        grid=(n_chunks,),
        in_specs=[ctok(RWKV_WIDTH)] * 8 + [_const_spec((1, RWKV_WIDTH))] * 2 + [_const_spec((CHUNK, CHUNK))],
        out_specs=ctok(RWKV_WIDTH),
        out_shape=bshape(RWKV_WIDTH),
        scratch_shapes=[pltpu.VMEM((RWKV_HEAD, RWKV_WIDTH), F32)],
        compiler_params=pltpu.CompilerParams(dimension_semantics=("arbitrary",),
                                             vmem_limit_bytes=VMEM_LIMIT_V7X),
        name="rwkv_chunk",
    )(rr, rk2, rv, rkk, rkb, rlw, rg, rbonus, row(p['rwkv_gn_w']), row(p['rwkv_gn_b']), tri)

    mtok = lambda w: pl.BlockSpec((tm_mix, w), lambda i: (i, 0))
    out = pl.pallas_call(
        _mix_kernel,
        grid=(t // tm_mix,),
        in_specs=[mtok(D_MODEL), mtok(GLA_WIDTH), mtok(RWKV_WIDTH),
                  _const_spec((D_MODEL, 2 * D_MODEL)), _const_spec((1, 2 * D_MODEL)),
                  _const_spec((GLA_WIDTH, D_MODEL)), _const_spec((RWKV_WIDTH, D_MODEL)),
                  _const_spec((D_MODEL, D_MODEL)), _const_spec((1, D_MODEL)), _const_spec((1, D_MODEL)),
                  _const_spec((D_MODEL, D_FF)), _const_spec((1, D_FF)),
                  _const_spec((D_FF, D_MODEL)), _const_spec((1, D_MODEL)),
                  _const_spec((1, D_MODEL)), _const_spec((1, D_MODEL))],
        out_specs=mtok(D_MODEL),
        out_shape=jax.ShapeDtypeStruct((t, D_MODEL), F32),
        compiler_params=pltpu.CompilerParams(dimension_semantics=("arbitrary",),
                                             vmem_limit_bytes=VMEM_LIMIT_V7X),
        name="merge_mlp",
    )(x2, o_a, o_b, p['w_merge'].astype(BF16), row(p['b_merge']),
      p['w_branch'][0].astype(BF16), p['w_branch'][1].astype(BF16), p['w_out'].astype(BF16),
      row(p['ln1_g']), row(p['ln1_b']), p['w_mlp_up'].astype(BF16), row(p['b_mlp_up']),
      p['w_mlp_down'].astype(BF16), row(p['b_mlp_down']), row(p['ln2_g']), row(p['ln2_b']))
    return out


def kernel(x, w_in, mu_shift, w_gk_up, b_gk, gla_norm_w, rwkv_w0, rwkv_w_up, rwkv_a0, rwkv_a_up,
           rwkv_g_up, rwkv_k_k, rwkv_k_a, rwkv_r_k, rwkv_gn_w, rwkv_gn_b, w_merge, b_merge, w_branch,
           w_out, ln1_g, ln1_b, w_mlp_up, b_mlp_up, w_mlp_down, b_mlp_down, ln2_g, ln2_b):
    bsz, seq, d = x.shape
    assert d == D_MODEL and seq % CHUNK == 0
    tm_proj = 512 if seq % 512 == 0 else CHUNK
    tm_mix = 256 if (bsz * seq) % 256 == 0 else CHUNK
    params = dict(w_in=w_in, mu_shift=mu_shift, w_gk_up=w_gk_up, b_gk=b_gk, gla_norm_w=gla_norm_w,
                  rwkv_w0=rwkv_w0, rwkv_w_up=rwkv_w_up, rwkv_a0=rwkv_a0, rwkv_a_up=rwkv_a_up,
                  rwkv_g_up=rwkv_g_up, rwkv_k_k=rwkv_k_k, rwkv_k_a=rwkv_k_a, rwkv_r_k=rwkv_r_k,
                  rwkv_gn_w=rwkv_gn_w, rwkv_gn_b=rwkv_gn_b, w_merge=w_merge, b_merge=b_merge,
                  w_branch=w_branch, w_out=w_out, ln1_g=ln1_g, ln1_b=ln1_b, w_mlp_up=w_mlp_up,
                  b_mlp_up=b_mlp_up, w_mlp_down=w_mlp_down, b_mlp_down=b_mlp_down, ln2_g=ln2_g, ln2_b=ln2_b)
    x2 = x.reshape(bsz * seq, d)
    for l in range(w_in.shape[0]):
        x2 = _layer(x2, bsz, seq, {n: a[l] for n, a in params.items()}, tm_proj, tm_mix)
    return x2.reshape(bsz, seq, d)
```

```python
import functools

import jax
import jax.numpy as jnp
import numpy as np
from jax import lax
from jax.experimental import pallas as pl
from jax.experimental.pallas import tpu as pltpu

F32 = jnp.float32
BF16 = jnp.bfloat16

D_MODEL = 1024
D_FF = 4 * D_MODEL
DEPTH = 1
CHUNK = 64
SUB = 16
GLA_HEADS, GLA_DK, GLA_DV = 4, 64, 128
GLA_GATE_RANK = 16
GLA_GATE_NORMALIZER = 16.0
GLA_NORM_EPS = 1e-5
RWKV_HEADS, RWKV_HEAD = 8, 64
RWKV_GN_EPS = 64e-5
L2_EPS = 1e-12
LN_EPS = 1e-5
ALPHA = (2.0 * DEPTH) ** 0.25
GLA_QK = GLA_HEADS * GLA_DK
GLA_WIDTH = GLA_HEADS * GLA_DV
RWKV_WIDTH = RWKV_HEADS * RWKV_HEAD
GLA_IN = 2 * GLA_QK + 2 * GLA_WIDTH + GLA_GATE_RANK
GLA_IN_PAD = 1664
RWKV_IN = 3 * RWKV_WIDTH + 64 + 64 + 128
LORA_OFF = 3 * RWKV_WIDTH
LORA_IN = 256
FF_CHUNK = 1024
RWKV_BLOCK = 256
VMEM_LIMIT_V7X = 56 * 1024 * 1024


def _dot(a, b):
    return jnp.dot(a, b, preferred_element_type=F32)


def _dot_nt(a, b):
    return lax.dot_general(a, b, (((1,), (1,)), ((), ())), preferred_element_type=F32)


def _dot_tn(a, b):
    return lax.dot_general(a, b, (((0,), (0,)), ((), ())), preferred_element_type=F32)


def _sigmoid(z):
    return 1.0 / (1.0 + jnp.exp(-z))


def _log_sigmoid(z):
    return jnp.minimum(z, 0.0) - jnp.log(1.0 + jnp.exp(-jnp.abs(z)))


def _split_bf16(a):
    hi = a.astype(BF16)
    lo = (a - hi.astype(F32)).astype(BF16)
    return hi, lo


def _const_spec(shape):
    nd = len(shape)
    return pl.BlockSpec(shape, lambda *_: (0,) * nd, pipeline_mode=pl.Buffered(1))


def _proj_kernel(x_ref, wg_ref, wr_ref, mu_ref, wgk_ref, bgk_ref, wlora_ref, w0_ref, a0_ref,
                 kk_ref, ka_ref, rk_ref, ones_ref,
                 q_ref, k_ref, v_ref, sg_ref, gk_ref,
                 r_ref, rk2_ref, rv_ref, kkn_ref, kb_ref, lw_ref, g_ref, bonus_ref,
                 carry_ref, *, tiles_per_seq):
    i = pl.program_id(0)
    tm = x_ref.shape[0]
    xb = x_ref[...].astype(BF16)

    hg = _dot(xb, wg_ref[...])
    q_ref[...] = (hg[:, 0:GLA_QK] * (GLA_DK ** -0.5)).astype(BF16)
    k_ref[...] = hg[:, GLA_QK:2 * GLA_QK].astype(BF16)
    v_ref[...] = hg[:, 2 * GLA_QK:2 * GLA_QK + GLA_WIDTH].astype(BF16)
    g_out = hg[:, 2 * GLA_QK + GLA_WIDTH:2 * GLA_QK + 2 * GLA_WIDTH]
    sg_ref[...] = (g_out * _sigmoid(g_out)).astype(BF16)
    z = _dot(hg[:, 1536:1664].astype(BF16), wgk_ref[...]) + bgk_ref[...]
    gk_ref[...] = _log_sigmoid(z) * (1.0 / GLA_GATE_NORMALIZER)

    hr = _dot(xb, wr_ref[...])

    @pl.when(i % tiles_per_seq == 0)
    def _():
        carry_ref[...] = jnp.zeros_like(carry_ref)

    row = lax.broadcasted_iota(jnp.int32, (tm, 1), 0)
    prev = jnp.where(row == 0, carry_ref[...], pltpu.roll(hr, 1, 0))
    carry_ref[...] = hr[tm - 1:tm, :]
    u = hr + (prev - hr) * mu_ref[...]

    r = u[:, 0:RWKV_WIDTH]
    k = u[:, RWKV_WIDTH:2 * RWKV_WIDTH]
    v = u[:, 2 * RWKV_WIDTH:3 * RWKV_WIDTH]
    low = u[:, LORA_OFF:LORA_OFF + LORA_IN]
    col = lax.broadcasted_iota(jnp.int32, (1, LORA_IN), 1)
    low = jnp.where(col < 64, jnp.tanh(low), jnp.where(col < 128, low, _sigmoid(low)))
    lo = _dot(low.astype(BF16), wlora_ref[...])
    lw_ref[...] = -float(np.exp(-0.5)) * _sigmoid(w0_ref[...] + lo[:, 0:RWKV_WIDTH])
    a = _sigmoid(a0_ref[...] + lo[:, RWKV_WIDTH:2 * RWKV_WIDTH])
    g_ref[...] = lo[:, 2 * RWKV_WIDTH:3 * RWKV_WIDTH].astype(BF16)

    ones = ones_ref[...]
    kk = k * kk_ref[...]
    ss = _dot((kk * kk).astype(BF16), ones)
    kkn = kk / jnp.maximum(jnp.sqrt(ss), L2_EPS)
    k2 = k * (1.0 + (a - 1.0) * ka_ref[...])
    bonus = _dot((r * k2 * rk_ref[...]).astype(BF16), ones) * v
    r_ref[...] = r.astype(BF16)
    rk2_ref[...] = k2.astype(BF16)
    rv_ref[...] = v.astype(BF16)
    kkn_ref[...] = kkn.astype(BF16)
    kb_ref[...] = (kkn * a).astype(BF16)
    bonus_ref[...] = bonus.astype(BF16)


def _gla_kernel(q_ref, k_ref, v_ref, sg_ref, gk_ref, nw_ref, tri_ref, e_ref, o_ref, s_ref, *, nc):
    i = pl.program_id(0)

    @pl.when(i % nc == 0)
    def _():
        s_ref[...] = jnp.zeros_like(s_ref)

    tri = tri_ref[...]
    g_hi, g_lo = _split_bf16(gk_ref[...])
    b = _dot(tri, g_hi) + _dot(tri, g_lo)
    q = q_ref[...].astype(F32)
    k = k_ref[...].astype(F32)
    v = v_ref[...].astype(F32)
    row = lax.broadcasted_iota(jnp.int32, (CHUNK, 1), 0)
    rmod = row % SUB

    e = e_ref[...]
    diag = jnp.zeros((CHUNK, GLA_WIDTH), F32)
    for d in range(SUB):
        ks = k if d == 0 else pltpu.roll(k, d, 0)
        bs = b if d == 0 else pltpu.roll(b, d, 0)
        vs = v if d == 0 else pltpu.roll(v, d, 0)
        xd = q * ks * jnp.exp(jnp.minimum(b - bs, 0.0))
        xd = jnp.where(rmod >= d, xd, 0.0)
        diag = diag + _dot(xd.astype(BF16), e) * vs

    b_last = b[CHUNK - 1:CHUNK, :]
    qe = (q * jnp.exp(b)).astype(BF16)
    kdec = (k * jnp.exp(b_last - b)).astype(BF16)
    s_all = s_ref[...]
    outs = []
    for h in range(GLA_HEADS):
        sl = slice(h * GLA_DK, (h + 1) * GLA_DK)
        vl = slice(h * GLA_DV, (h + 1) * GLA_DV)
        qh, kh, bh = q[:, sl], k[:, sl], b[:, sl]
        vh = v[:, vl].astype(BF16)
        rows = [jnp.zeros((SUB, CHUNK), F32)]
        for blk in range(1, CHUNK // SUB):
            lo_r = blk * SUB
            ref = bh[lo_r - 1:lo_r, :]
            qi = qh[lo_r:lo_r + SUB, :] * jnp.exp(bh[lo_r:lo_r + SUB, :] - ref)
            ki = jnp.where(row < lo_r, kh * jnp.exp(jnp.minimum(ref - bh, 0.0)), 0.0)
            rows.append(_dot_nt(qi.astype(BF16), ki.astype(BF16)))
        sc = jnp.concatenate(rows, axis=0)
        sh = s_all[:, sl]
        oh = _dot_nt(qe[:, sl], sh.astype(BF16)) + _dot(sc.astype(BF16), vh) + diag[:, vl]
        ms = jnp.mean(oh * oh, axis=-1, keepdims=True)
        outs.append(oh * lax.rsqrt(ms + GLA_NORM_EPS))
        s_ref[:, sl] = sh * jnp.exp(b_last[:, sl]) + _dot_tn(vh, kdec[:, sl])
    o = jnp.concatenate(outs, axis=-1) * nw_ref[...]
    o_ref[...] = (o * sg_ref[...].astype(F32)).astype(BF16)


def _rwkv_kernel(r_ref, k_ref, v_ref, kk_ref, kb_ref, lw_ref, g_ref, bonus_ref, gnw_ref, gnb_ref,
                 tri_ref, ones_ref, o_ref, s_ref, y_ref, *, steps_per_seq, n_sub):
    i = pl.program_id(0)

    @pl.when(i % steps_per_seq == 0)
    def _():
        s_ref[...] = jnp.zeros_like(s_ref)

    heads = range(RWKV_HEADS)
    hsl = [slice(h * RWKV_HEAD, (h + 1) * RWKV_HEAD) for h in heads]
    lw = lw_ref[...]
    l_hi, l_lo = _split_bf16(lw)
    tri = tri_ref[...]
    c = _dot(tri, l_hi) + _dot(tri, l_lo)
    e_c = jnp.exp(c)
    e_nc = jnp.exp(-c)
    r = r_ref[...].astype(F32)
    k = k_ref[...].astype(F32)
    kk = kk_ref[...].astype(F32)
    kb = kb_ref[...].astype(F32)
    v_all = v_ref[...]
    at_all = (-kk * jnp.exp(c - lw)).astype(BF16)
    rt_all = (r * e_c).astype(BF16)
    bt_all = (kb * e_nc).astype(BF16)
    kt_all = (k * e_nc).astype(BF16)

    row = lax.broadcasted_iota(jnp.int32, (CHUNK, CHUNK), 0)
    col = lax.broadcasted_iota(jnp.int32, (CHUNK, CHUNK), 1)
    strict = row > col
    incl = row >= col
    eye = (row == col).astype(F32)

    subs = range(n_sub)
    rows = [slice(g * CHUNK, (g + 1) * CHUNK) for g in subs]
    chains = [(g, h) for g in subs for h in heads]
    at = {(g, h): at_all[rows[g], hsl[h]] for g, h in chains}
    rt = {(g, h): rt_all[rows[g], hsl[h]] for g, h in chains}
    vv = {(g, h): v_all[rows[g], hsl[h]] for g, h in chains}
    a_mat = {(g, h): _dot_nt(jnp.concatenate([at[g, h], rt[g, h]], axis=0),
                             jnp.concatenate([bt_all[rows[g], hsl[h]], kt_all[rows[g], hsl[h]]], axis=0))
             for g, h in chains}
    a_ab = {ch: jnp.where(strict, a_mat[ch][0:CHUNK, 0:CHUNK], 0.0) for ch in chains}
    a_ak = {ch: jnp.where(strict, a_mat[ch][0:CHUNK, CHUNK:], 0.0).astype(BF16) for ch in chains}
    a_rb = {ch: jnp.where(incl, a_mat[ch][CHUNK:, 0:CHUNK], 0.0).astype(BF16) for ch in chains}
    a_rk = {ch: jnp.where(incl, a_mat[ch][CHUNK:, CHUNK:], 0.0).astype(BF16) for ch in chains}
    av = {ch: _dot(jnp.concatenate([a_ak[ch], a_rk[ch]], axis=0), vv[ch]) for ch in chains}
    akv = {ch: av[ch][0:CHUNK].astype(BF16) for ch in chains}
    y_in = {ch: av[ch][CHUNK:] for ch in chains}
    t_inv = {ch: eye + a_ab[ch] for ch in chains}
    pw = {ch: a_ab[ch].astype(BF16) for ch in chains}
    pw = {ch: _dot(pw[ch], pw[ch]).astype(BF16) for ch in chains}
    for _ in range(4):
        pt = {ch: _dot(jnp.concatenate([pw[ch], t_inv[ch].astype(BF16)], axis=0), pw[ch]) for ch in chains}
        pw = {ch: pt[ch][0:CHUNK].astype(BF16) for ch in chains}
        t_inv = {ch: t_inv[ch] + pt[ch][CHUNK:] for ch in chains}
    t_inv = {ch: t_inv[ch] + _dot(t_inv[ch].astype(BF16), pw[ch]) for ch in chains}
    wu = {ch: _dot(t_inv[ch].astype(BF16), jnp.concatenate([at[ch], akv[ch]], axis=1)) for ch in chains}
    wr_til = {ch: jnp.concatenate([wu[ch][:, 0:RWKV_HEAD].astype(BF16), rt[ch]], axis=0) for ch in chains}
    u_til = {ch: wu[ch][:, RWKV_HEAD:] for ch in chains}

    state = [s_ref[:, hsl[h]] for h in heads]
    for g in subs:
        c_last = c[(g + 1) * CHUNK - 1:(g + 1) * CHUNK, :]
        e_rem = jnp.exp(c_last - c[rows[g], :])
        bh_all = (kb[rows[g], :] * e_rem).astype(BF16)
        kh_all = (k[rows[g], :] * e_rem).astype(BF16)
        e_last = jnp.exp(c_last)
        ws = [_dot_nt(wr_til[g, h], state[h].astype(BF16)) for h in heads]
        u = [(ws[h][0:CHUNK] + u_til[g, h]).astype(BF16) for h in heads]
        for h in heads:
            y_ref[rows[g], hsl[h]] = ws[h][CHUNK:] + _dot(a_rb[g, h], u[h]) + y_in[g, h]
        state = [state[h] * e_last[:, hsl[h]] + _dot_tn(
            jnp.concatenate([u[h], vv[g, h]], axis=0),
            jnp.concatenate([bh_all[:, hsl[h]], kh_all[:, hsl[h]]], axis=0)) for h in heads]
    for h in heads:
        s_ref[:, hsl[h]] = state[h]

    ones = ones_ref[...]
    y = y_ref[...]
    mu = _dot(y.astype(BF16), ones) * (1.0 / RWKV_HEAD)
    yc = y - mu
    var = _dot((yc * yc).astype(BF16), ones) * (1.0 / RWKV_HEAD)
    yn = yc * lax.rsqrt(var + RWKV_GN_EPS) * gnw_ref[...] + gnb_ref[...]
    o_ref[...] = ((yn + bonus_ref[...].astype(F32)) * g_ref[...].astype(F32)).astype(BF16)


def _layernorm(h, g, b):
    mu = jnp.mean(h, axis=-1, keepdims=True)
    hc = h - mu
    var = jnp.mean(hc * hc, axis=-1, keepdims=True)
    return hc * lax.rsqrt(var + LN_EPS) * g + b


def _mix_kernel(x_ref, oa_ref, ob_ref, wm_ref, bm_ref, wba_ref, wbb_ref, wout_ref, ln1g_ref, ln1b_ref,
                w1_ref, b1_ref, w2_ref, b2_ref, ln2g_ref, ln2b_ref, out_ref):
    x = x_ref[...]
    xb = x.astype(BF16)
    ya = _dot(oa_ref[...], wba_ref[...])
    yb = _dot(ob_ref[...], wbb_ref[...])
    ga = _sigmoid(_dot(xb, wm_ref[:, 0:D_MODEL]) + bm_ref[:, 0:D_MODEL])
    m = ga * ya
    gb = _sigmoid(_dot(xb, wm_ref[:, D_MODEL:]) + bm_ref[:, D_MODEL:])
    m = m + gb * yb
    mix = _dot(m.astype(BF16), wout_ref[...])
    x1 = _layernorm(ALPHA * x + mix, ln1g_ref[...], ln1b_ref[...])
    x1b = x1.astype(BF16)
    acc = jnp.zeros_like(x1)
    for c in range(D_FF // FF_CHUNK):
        cs = slice(c * FF_CHUNK, (c + 1) * FF_CHUNK)
        h = jnp.maximum(_dot(x1b, w1_ref[:, cs]) + b1_ref[:, cs], 0.0)
        acc = acc + _dot((h * h).astype(BF16), w2_ref[cs, :])
    out_ref[...] = _layernorm(ALPHA * x1 + acc + b2_ref[...], ln2g_ref[...], ln2b_ref[...])


def _block_ones(n, blk):
    idx = np.arange(n) // blk
    return jnp.asarray(idx[:, None] == idx[None, :], BF16)


def _layer(x2, bsz, seq, p, tm_proj, tm_mix):
    t = bsz * seq
    row = lambda a: a.reshape(1, -1).astype(F32)

    w_in = p['w_in']
    wg = jnp.pad(w_in[:, :GLA_IN], ((0, 0), (0, GLA_IN_PAD - GLA_IN))).astype(BF16)
    wr = w_in[:, GLA_IN:].astype(BF16)
    wgk = jnp.pad(p['w_gk_up'], ((0, 128 - GLA_GATE_RANK), (0, 0))).astype(BF16)
    wlora = jnp.zeros((LORA_IN, 3 * RWKV_WIDTH), F32)
    wlora = wlora.at[0:64, 0:RWKV_WIDTH].set(p['rwkv_w_up'])
    wlora = wlora.at[64:128, RWKV_WIDTH:2 * RWKV_WIDTH].set(p['rwkv_a_up'])
    wlora = wlora.at[128:256, 2 * RWKV_WIDTH:].set(p['rwkv_g_up']).astype(BF16)
    ones_r = _block_ones(RWKV_WIDTH, RWKV_HEAD)

    n_tiles = t // tm_proj
    tok = lambda w: pl.BlockSpec((tm_proj, w), lambda i: (i, 0))
    bshape = lambda w: jax.ShapeDtypeStruct((t, w), BF16)
    proj_out = pl.pallas_call(
        functools.partial(_proj_kernel, tiles_per_seq=seq // tm_proj),
        grid=(n_tiles,),
        in_specs=[tok(D_MODEL), _const_spec((D_MODEL, GLA_IN_PAD)), _const_spec((D_MODEL, RWKV_IN)),
                  _const_spec((1, RWKV_IN)), _const_spec((128, GLA_QK)), _const_spec((1, GLA_QK)),
                  _const_spec((LORA_IN, 3 * RWKV_WIDTH))] + [_const_spec((1, RWKV_WIDTH))] * 5
                 + [_const_spec((RWKV_WIDTH, RWKV_WIDTH))],
        out_specs=[tok(GLA_QK), tok(GLA_QK), tok(GLA_WIDTH), tok(GLA_WIDTH), tok(GLA_QK)]
                  + [tok(RWKV_WIDTH)] * 8,
        out_shape=[bshape(GLA_QK), bshape(GLA_QK), bshape(GLA_WIDTH), bshape(GLA_WIDTH),
                   jax.ShapeDtypeStruct((t, GLA_QK), F32)]
                  + [bshape(RWKV_WIDTH)] * 5 + [jax.ShapeDtypeStruct((t, RWKV_WIDTH), F32)]
                  + [bshape(RWKV_WIDTH)] * 2,
        scratch_shapes=[pltpu.VMEM((1, RWKV_IN), F32)],
        compiler_params=pltpu.CompilerParams(dimension_semantics=("arbitrary",),
                                             vmem_limit_bytes=VMEM_LIMIT_V7X),
        name="proj_prep",
    )(x2, wg, wr, row(p['mu_shift']), wgk, row(p['b_gk']), wlora, row(p['rwkv_w0']), row(p['rwkv_a0']),
      row(p['rwkv_k_k']), row(p['rwkv_k_a']), row(p['rwkv_r_k']), ones_r)
    gq, gk_, gv, gsg, ggk, rr, rk2, rv, rkk, rkb, rlw, rg, rbonus = proj_out

    nc = seq // CHUNK
    n_chunks = t // CHUNK
    ctok = lambda w: pl.BlockSpec((CHUNK, w), lambda i: (i, 0))
    tri = jnp.asarray(np.tril(np.ones((CHUNK, CHUNK), np.float32)), BF16)
    hid = np.arange(GLA_QK) // GLA_DK
    vid = np.arange(GLA_WIDTH) // GLA_DV
    e_mat = jnp.asarray(hid[:, None] == vid[None, :], BF16)
    o_a = pl.pallas_call(
        functools.partial(_gla_kernel, nc=nc),
        grid=(n_chunks,),
        in_specs=[ctok(GLA_QK), ctok(GLA_QK), ctok(GLA_WIDTH), ctok(GLA_WIDTH), ctok(GLA_QK),
                  _const_spec((1, GLA_WIDTH)), _const_spec((CHUNK, CHUNK)), _const_spec((GLA_QK, GLA_WIDTH))],
        out_specs=ctok(GLA_WIDTH),
        out_shape=bshape(GLA_WIDTH),
        scratch_shapes=[pltpu.VMEM((GLA_DV, GLA_QK), F32)],
        compiler_params=pltpu.CompilerParams(dimension_semantics=("arbitrary",),
                                             vmem_limit_bytes=VMEM_LIMIT_V7X),
        name="gla_chunk",
    )(gq, gk_, gv, gsg, ggk, row(jnp.tile(p['gla_norm_w'], GLA_HEADS)), tri, e_mat)

    rtok = lambda w: pl.BlockSpec((RWKV_BLOCK, w), lambda i: (i, 0))
    n_sub = RWKV_BLOCK // CHUNK
    tri_r = jnp.asarray(np.kron(np.eye(n_sub), np.tril(np.ones((CHUNK, CHUNK)))), BF16)
    o_b = pl.pallas_call(
        functools.partial(_rwkv_kernel, steps_per_seq=seq // RWKV_BLOCK, n_sub=n_sub),
        grid=(t // RWKV_BLOCK,),
        in_specs=[rtok(RWKV_WIDTH)] * 8 + [_const_spec((1, RWKV_WIDTH))] * 2
                 + [_const_spec((RWKV_BLOCK, RWKV_BLOCK)), _const_spec((RWKV_WIDTH, RWKV_WIDTH))],
        out_specs=rtok(RWKV_WIDTH),
        out_shape=bshape(RWKV_WIDTH),
        scratch_shapes=[pltpu.VMEM((RWKV_HEAD, RWKV_WIDTH), F32), pltpu.VMEM((RWKV_BLOCK, RWKV_WIDTH), F32)],
        compiler_params=pltpu.CompilerParams(dimension_semantics=("arbitrary",),
                                             vmem_limit_bytes=VMEM_LIMIT_V7X),
        name="rwkv_chunk",
    )(rr, rk2, rv, rkk, rkb, rlw, rg, rbonus, row(p['rwkv_gn_w']), row(p['rwkv_gn_b']), tri_r, ones_r)

    mtok = lambda w: pl.BlockSpec((tm_mix, w), lambda i: (i, 0))
    out = pl.pallas_call(
        _mix_kernel,
        grid=(t // tm_mix,),
        in_specs=[mtok(D_MODEL), mtok(GLA_WIDTH), mtok(RWKV_WIDTH),
                  _const_spec((D_MODEL, 2 * D_MODEL)), _const_spec((1, 2 * D_MODEL)),
                  _const_spec((GLA_WIDTH, D_MODEL)), _const_spec((RWKV_WIDTH, D_MODEL)),
                  _const_spec((D_MODEL, D_MODEL)), _const_spec((1, D_MODEL)), _const_spec((1, D_MODEL)),
                  _const_spec((D_MODEL, D_FF)), _const_spec((1, D_FF)),
                  _const_spec((D_FF, D_MODEL)), _const_spec((1, D_MODEL)),
                  _const_spec((1, D_MODEL)), _const_spec((1, D_MODEL))],
        out_specs=mtok(D_MODEL),
        out_shape=jax.ShapeDtypeStruct((t, D_MODEL), F32),
        compiler_params=pltpu.CompilerParams(dimension_semantics=("arbitrary",),
                                             vmem_limit_bytes=VMEM_LIMIT_V7X),
        name="merge_mlp",
    )(x2, o_a, o_b, p['w_merge'].astype(BF16), row(p['b_merge']),
      p['w_branch'][0].astype(BF16), p['w_branch'][1].astype(BF16), p['w_out'].astype(BF16),
      row(p['ln1_g']), row(p['ln1_b']), p['w_mlp_up'].astype(BF16), row(p['b_mlp_up']),
      p['w_mlp_down'].astype(BF16), row(p['b_mlp_down']), row(p['ln2_g']), row(p['ln2_b']))
    return out


def kernel(x, w_in, mu_shift, w_gk_up, b_gk, gla_norm_w, rwkv_w0, rwkv_w_up, rwkv_a0, rwkv_a_up,
           rwkv_g_up, rwkv_k_k, rwkv_k_a, rwkv_r_k, rwkv_gn_w, rwkv_gn_b, w_merge, b_merge, w_branch,
           w_out, ln1_g, ln1_b, w_mlp_up, b_mlp_up, w_mlp_down, b_mlp_down, ln2_g, ln2_b):
    bsz, seq, d = x.shape
    assert d == D_MODEL and seq % CHUNK == 0
    tm_proj = 512 if seq % 512 == 0 else CHUNK
    tm_mix = 256 if (bsz * seq) % 256 == 0 else CHUNK
    params = dict(w_in=w_in, mu_shift=mu_shift, w_gk_up=w_gk_up, b_gk=b_gk, gla_norm_w=gla_norm_w,
                  rwkv_w0=rwkv_w0, rwkv_w_up=rwkv_w_up, rwkv_a0=rwkv_a0, rwkv_a_up=rwkv_a_up,
                  rwkv_g_up=rwkv_g_up, rwkv_k_k=rwkv_k_k, rwkv_k_a=rwkv_k_a, rwkv_r_k=rwkv_r_k,
                  rwkv_gn_w=rwkv_gn_w, rwkv_gn_b=rwkv_gn_b, w_merge=w_merge, b_merge=b_merge,
                  w_branch=w_branch, w_out=w_out, ln1_g=ln1_g, ln1_b=ln1_b, w_mlp_up=w_mlp_up,
                  b_mlp_up=b_mlp_up, w_mlp_down=w_mlp_down, b_mlp_down=b_mlp_down, ln2_g=ln2_g, ln2_b=ln2_b)
    x2 = x.reshape(bsz * seq, d)
    for l in range(w_in.shape[0]):
        x2 = _layer(x2, bsz, seq, {n: a[l] for n, a in params.items()}, tm_proj, tm_mix)
    return x2.reshape(bsz, seq, d)
```

```python
import functools

import jax
import jax.numpy as jnp
import numpy as np
from jax import lax
from jax.experimental import pallas as pl
from jax.experimental.pallas import tpu as pltpu

F32 = jnp.float32
BF16 = jnp.bfloat16

D_MODEL = 1024
D_FF = 4 * D_MODEL
DEPTH = 1
CHUNK = 64
SUB = 16
GLA_HEADS, GLA_DK, GLA_DV = 4, 64, 128
GLA_GATE_RANK = 16
GLA_GATE_NORMALIZER = 16.0
GLA_NORM_EPS = 1e-5
RWKV_HEADS, RWKV_HEAD = 8, 64
RWKV_GN_EPS = 64e-5
L2_EPS = 1e-12
LN_EPS = 1e-5
ALPHA = (2.0 * DEPTH) ** 0.25
GLA_QK = GLA_HEADS * GLA_DK
GLA_WIDTH = GLA_HEADS * GLA_DV
RWKV_WIDTH = RWKV_HEADS * RWKV_HEAD
GLA_IN = 2 * GLA_QK + 2 * GLA_WIDTH + GLA_GATE_RANK
GLA_IN_PAD = 1664
RWKV_IN = 3 * RWKV_WIDTH + 64 + 64 + 128
LORA_OFF = 3 * RWKV_WIDTH
LORA_IN = 256
FF_CHUNK = 1024
RWKV_BLOCK = 256
GLA_BLOCK = 256
LOG2_E = float(np.log2(np.e))
GLA_SAFE_LOG2 = 96.0
VMEM_LIMIT_V7X = 56 * 1024 * 1024


def _dot(a, b):
    return jnp.dot(a, b, preferred_element_type=F32)


def _dot_nt(a, b):
    return lax.dot_general(a, b, (((1,), (1,)), ((), ())), preferred_element_type=F32)


def _dot_tn(a, b):
    return lax.dot_general(a, b, (((0,), (0,)), ((), ())), preferred_element_type=F32)


def _sigmoid(z):
    return 1.0 / (1.0 + jnp.exp(-z))


def _log_sigmoid(z):
    return jnp.minimum(z, 0.0) - jnp.log(1.0 + jnp.exp(-jnp.abs(z)))


def _split_bf16(a):
    hi = a.astype(BF16)
    lo = (a - hi.astype(F32)).astype(BF16)
    return hi, lo


def _const_spec(shape):
    nd = len(shape)
    return pl.BlockSpec(shape, lambda *_: (0,) * nd, pipeline_mode=pl.Buffered(1))


def _proj_kernel(x_ref, wg_ref, wr_ref, mu_ref, wgk_ref, bgk_ref, wlora_ref, w0_ref, a0_ref,
                 kk_ref, ka_ref, rk_ref, ones_ref,
                 q_ref, k_ref, v_ref, sg_ref, gk_ref,
                 r_ref, rk2_ref, rv_ref, kkn_ref, kb_ref, lw_ref, g_ref, bonus_ref,
                 carry_ref, *, tiles_per_seq):
    i = pl.program_id(0)
    tm = x_ref.shape[0]
    xb = x_ref[...].astype(BF16)

    hg = _dot(xb, wg_ref[...])
    q_ref[...] = (hg[:, 0:GLA_QK] * (GLA_DK ** -0.5)).astype(BF16)
    k_ref[...] = hg[:, GLA_QK:2 * GLA_QK].astype(BF16)
    v_ref[...] = hg[:, 2 * GLA_QK:2 * GLA_QK + GLA_WIDTH].astype(BF16)
    g_out = hg[:, 2 * GLA_QK + GLA_WIDTH:2 * GLA_QK + 2 * GLA_WIDTH]
    sg_ref[...] = (g_out * _sigmoid(g_out)).astype(BF16)
    z = _dot(hg[:, 1536:1664].astype(BF16), wgk_ref[...]) + bgk_ref[...]
    gk_ref[...] = _log_sigmoid(z) * (1.0 / GLA_GATE_NORMALIZER)

    hr = _dot(xb, wr_ref[...])

    @pl.when(i % tiles_per_seq == 0)
    def _():
        carry_ref[...] = jnp.zeros_like(carry_ref)

    row = lax.broadcasted_iota(jnp.int32, (tm, 1), 0)
    prev = jnp.where(row == 0, carry_ref[...], pltpu.roll(hr, 1, 0))
    carry_ref[...] = hr[tm - 1:tm, :]
    u = hr + (prev - hr) * mu_ref[...]

    r = u[:, 0:RWKV_WIDTH]
    k = u[:, RWKV_WIDTH:2 * RWKV_WIDTH]
    v = u[:, 2 * RWKV_WIDTH:3 * RWKV_WIDTH]
    low = u[:, LORA_OFF:LORA_OFF + LORA_IN]
    col = lax.broadcasted_iota(jnp.int32, (1, LORA_IN), 1)
    low = jnp.where(col < 64, jnp.tanh(low), jnp.where(col < 128, low, _sigmoid(low)))
    lo = _dot(low.astype(BF16), wlora_ref[...])
    lw_ref[...] = -float(np.exp(-0.5)) * _sigmoid(w0_ref[...] + lo[:, 0:RWKV_WIDTH])
    a = _sigmoid(a0_ref[...] + lo[:, RWKV_WIDTH:2 * RWKV_WIDTH])
    g_ref[...] = lo[:, 2 * RWKV_WIDTH:3 * RWKV_WIDTH].astype(BF16)

    ones = ones_ref[...]
    kk = k * kk_ref[...]
    ss = _dot((kk * kk).astype(BF16), ones)
    kkn = kk / jnp.maximum(jnp.sqrt(ss), L2_EPS)
    k2 = k * (1.0 + (a - 1.0) * ka_ref[...])
    bonus = _dot((r * k2 * rk_ref[...]).astype(BF16), ones) * v
    r_ref[...] = r.astype(BF16)
    rk2_ref[...] = k2.astype(BF16)
    rv_ref[...] = v.astype(BF16)
    kkn_ref[...] = kkn.astype(BF16)
    kb_ref[...] = (kkn * a).astype(BF16)
    bonus_ref[...] = bonus.astype(BF16)


def _gla_kernel(q_ref, k_ref, v_ref, sg_ref, gk_ref, nw_ref, tri_ref, ebig_ref, o_ref,
                s_ref, sc_ref, kp_ref, bp_ref, *, steps_per_seq, n_sub):
    i = pl.program_id(0)

    @pl.when(i % steps_per_seq == 0)
    def _():
        s_ref[...] = jnp.zeros_like(s_ref)

    blk = n_sub * CHUNK
    heads = range(GLA_HEADS)
    subs = range(n_sub)
    hsl = [slice(h * GLA_DK, (h + 1) * GLA_DK) for h in heads]
    vsl = [slice(h * GLA_DV, (h + 1) * GLA_DV) for h in heads]
    rows = [slice(g * CHUNK, (g + 1) * CHUNK) for g in subs]
    pairs = [(g, h) for g in subs for h in heads]

    g_hi, g_lo = _split_bf16(gk_ref[...])
    tri = tri_ref[...]
    b = (_dot(tri, g_hi) + _dot(tri, g_lo)) * LOG2_E
    q = q_ref[...].astype(F32)
    k = k_ref[...].astype(F32)
    v_all = v_ref[...]

    qe = (q * jnp.exp2(b)).astype(BF16)
    row_c = lax.broadcasted_iota(jnp.int32, (CHUNK, 1), 0)
    col_c = lax.broadcasted_iota(jnp.int32, (1, CHUNK), 1)
    safe = jnp.min(b) >= -GLA_SAFE_LOG2

    @pl.when(safe)
    def _():
        kn = (k * jnp.exp2(-b)).astype(BF16)
        raw = {gh: _dot_nt(qe[rows[gh[0]], hsl[gh[1]]], kn[rows[gh[0]], hsl[gh[1]]]) for gh in pairs}
        sc = {gh: jnp.where(row_c >= col_c, raw[gh], 0.0).astype(BF16) for gh in pairs}
        for g, h in pairs:
            sc_ref[rows[g], vsl[h]] = _dot(sc[g, h], v_all[rows[g], vsl[h]])

    @pl.when(jnp.logical_not(safe))
    def _():
        kp_ref[0:SUB, :] = jnp.zeros((SUB, GLA_QK), F32)
        bp_ref[0:SUB, :] = jnp.zeros((SUB, GLA_QK), F32)
        kp_ref[SUB:SUB + blk, :] = k
        bp_ref[SUB:SUB + blk, :] = b
        rmod = lax.broadcasted_iota(jnp.int32, (blk, 1), 0) % SUB
        xs = [(q * k).astype(BF16)]
        for d in range(1, SUB):
            ks = kp_ref[SUB - d:SUB - d + blk, :]
            bs = bp_ref[SUB - d:SUB - d + blk, :]
            xs.append(jnp.where(rmod >= d, q * ks * jnp.exp2(b - bs), 0.0).astype(BF16))
        s_diag = _dot(jnp.concatenate(xs, axis=1), ebig_ref[...])
        lane = lax.broadcasted_iota(jnp.int32, (1, 128), 1)
        for g in subs:
            bg = b[rows[g], :]
            for h in heads:
                qh, kh, bh = q[rows[g], hsl[h]], k[rows[g], hsl[h]], bg[:, hsl[h]]
                md = jnp.where((lane >= SUB * h) & (lane < SUB * (h + 1)), s_diag[rows[g], :], 0.0)
                sc_rows = [jnp.zeros((SUB, CHUNK), F32)]
                for sb in range(1, CHUNK // SUB):
                    lo_r = sb * SUB
                    ref = bh[lo_r - 1:lo_r, :]
                    qi = qh[lo_r:lo_r + SUB, :] * jnp.exp2(bh[lo_r:lo_r + SUB, :] - ref)
                    ki = jnp.where(row_c < lo_r, kh * jnp.exp2(jnp.minimum(ref - bh, 0.0)), 0.0)
                    sc_rows.append(_dot_nt(qi.astype(BF16), ki.astype(BF16)))
                sc = jnp.concatenate(sc_rows, axis=0) + pltpu.roll(
                    md, 128 - (SUB - 1) - SUB * h, 1, stride=1, stride_axis=0)[:, 0:CHUNK]
                sc_ref[rows[g], vsl[h]] = _dot(sc.astype(BF16), v_all[rows[g], vsl[h]])

    b_last = {g: b[(g + 1) * CHUNK - 1:(g + 1) * CHUNK, :] for g in subs}
    kdec = {g: (k[rows[g], :] * jnp.exp2(b_last[g] - b[rows[g], :])).astype(BF16) for g in subs}
    kv = {(g, h): _dot_tn(v_all[rows[g], vsl[h]], kdec[g][:, hsl[h]]) for g, h in pairs}

    state = [s_ref[:, hsl[h]] for h in heads]
    entering = {}
    for g in subs:
        e_last = jnp.exp2(b_last[g])
        for h in heads:
            entering[g, h] = state[h].astype(BF16)
            state[h] = state[h] * e_last[:, hsl[h]] + kv[g, h]
    for h in heads:
        s_ref[:, hsl[h]] = state[h]
    inter = {(g, h): _dot_nt(qe[rows[g], hsl[h]], entering[g, h]) for g, h in pairs}
    for g in subs:
        outs = []
        for h in heads:
            oh = inter[g, h] + sc_ref[rows[g], vsl[h]]
            ms = jnp.mean(oh * oh, axis=-1, keepdims=True)
            outs.append(oh * lax.rsqrt(ms + GLA_NORM_EPS))
        o = jnp.concatenate(outs, axis=-1) * nw_ref[...]
        o_ref[rows[g], :] = (o * sg_ref[rows[g], :].astype(F32)).astype(BF16)


def _rwkv_kernel(r_ref, k_ref, v_ref, kk_ref, kb_ref, lw_ref, g_ref, bonus_ref, gnw_ref, gnb_ref,
                 tri_ref, ones_ref, o_ref, s_ref, y_ref, *, steps_per_seq, n_sub):
    i = pl.program_id(0)

    @pl.when(i % steps_per_seq == 0)
    def _():
        s_ref[...] = jnp.zeros_like(s_ref)

    heads = range(RWKV_HEADS)
    hsl = [slice(h * RWKV_HEAD, (h + 1) * RWKV_HEAD) for h in heads]
    lw = lw_ref[...]
    l_hi, l_lo = _split_bf16(lw)
    tri = tri_ref[...]
    c = _dot(tri, l_hi) + _dot(tri, l_lo)
    e_c = jnp.exp(c)
    e_nc = jnp.exp(-c)
    r = r_ref[...].astype(F32)
    k = k_ref[...].astype(F32)
    kk = kk_ref[...].astype(F32)
    kb = kb_ref[...].astype(F32)
    v_all = v_ref[...]
    at_all = (-kk * jnp.exp(c - lw)).astype(BF16)
    rt_all = (r * e_c).astype(BF16)
    bt_all = (kb * e_nc).astype(BF16)
    kt_all = (k * e_nc).astype(BF16)

    row = lax.broadcasted_iota(jnp.int32, (CHUNK, CHUNK), 0)
    col = lax.broadcasted_iota(jnp.int32, (CHUNK, CHUNK), 1)
    strict = row > col
    incl = row >= col
    eye = (row == col).astype(F32)

    subs = range(n_sub)
    rows = [slice(g * CHUNK, (g + 1) * CHUNK) for g in subs]
    chains = [(g, h) for g in subs for h in heads]
    at = {(g, h): at_all[rows[g], hsl[h]] for g, h in chains}
    rt = {(g, h): rt_all[rows[g], hsl[h]] for g, h in chains}
    vv = {(g, h): v_all[rows[g], hsl[h]] for g, h in chains}
    a_mat = {(g, h): _dot_nt(jnp.concatenate([at[g, h], rt[g, h]], axis=0),
                             jnp.concatenate([bt_all[rows[g], hsl[h]], kt_all[rows[g], hsl[h]]], axis=0))
             for g, h in chains}
    a_ab = {ch: jnp.where(strict, a_mat[ch][0:CHUNK, 0:CHUNK], 0.0) for ch in chains}
    a_ak = {ch: jnp.where(strict, a_mat[ch][0:CHUNK, CHUNK:], 0.0).astype(BF16) for ch in chains}
    a_rb = {ch: jnp.where(incl, a_mat[ch][CHUNK:, 0:CHUNK], 0.0).astype(BF16) for ch in chains}
    a_rk = {ch: jnp.where(incl, a_mat[ch][CHUNK:, CHUNK:], 0.0).astype(BF16) for ch in chains}
    av = {ch: _dot(jnp.concatenate([a_ak[ch], a_rk[ch]], axis=0), vv[ch]) for ch in chains}
    akv = {ch: av[ch][0:CHUNK].astype(BF16) for ch in chains}
    y_in = {ch: av[ch][CHUNK:] for ch in chains}
    t_inv = {ch: eye + a_ab[ch] for ch in chains}
    pw = {ch: a_ab[ch].astype(BF16) for ch in chains}
    pw = {ch: _dot(pw[ch], pw[ch]).astype(BF16) for ch in chains}
    for _ in range(4):
        pt = {ch: _dot(jnp.concatenate([pw[ch], t_inv[ch].astype(BF16)], axis=0), pw[ch]) for ch in chains}
        pw = {ch: pt[ch][0:CHUNK].astype(BF16) for ch in chains}
        t_inv = {ch: t_inv[ch] + pt[ch][CHUNK:] for ch in chains}
    t_inv = {ch: t_inv[ch] + _dot(t_inv[ch].astype(BF16), pw[ch]) for ch in chains}
    wu = {ch: _dot(t_inv[ch].astype(BF16), jnp.concatenate([at[ch], akv[ch]], axis=1)) for ch in chains}
    wr_til = {ch: jnp.concatenate([wu[ch][:, 0:RWKV_HEAD].astype(BF16), rt[ch]], axis=0) for ch in chains}
    u_til = {ch: wu[ch][:, RWKV_HEAD:] for ch in chains}

    state = [s_ref[:, hsl[h]] for h in heads]
    for g in subs:
        c_last = c[(g + 1) * CHUNK - 1:(g + 1) * CHUNK, :]
        e_rem = jnp.exp(c_last - c[rows[g], :])
        bh_all = (kb[rows[g], :] * e_rem).astype(BF16)
        kh_all = (k[rows[g], :] * e_rem).astype(BF16)
        e_last = jnp.exp(c_last)
        ws = [_dot_nt(wr_til[g, h], state[h].astype(BF16)) for h in heads]
        u = [(ws[h][0:CHUNK] + u_til[g, h]).astype(BF16) for h in heads]
        for h in heads:
            y_ref[rows[g], hsl[h]] = ws[h][CHUNK:] + _dot(a_rb[g, h], u[h]) + y_in[g, h]
        state = [state[h] * e_last[:, hsl[h]] + _dot_tn(
            jnp.concatenate([u[h], vv[g, h]], axis=0),
            jnp.concatenate([bh_all[:, hsl[h]], kh_all[:, hsl[h]]], axis=0)) for h in heads]
    for h in heads:
        s_ref[:, hsl[h]] = state[h]

    ones = ones_ref[...]
    y = y_ref[...]
    mu = _dot(y.astype(BF16), ones) * (1.0 / RWKV_HEAD)
    yc = y - mu
    var = _dot((yc * yc).astype(BF16), ones) * (1.0 / RWKV_HEAD)
    yn = yc * lax.rsqrt(var + RWKV_GN_EPS) * gnw_ref[...] + gnb_ref[...]
    o_ref[...] = ((yn + bonus_ref[...].astype(F32)) * g_ref[...].astype(F32)).astype(BF16)


def _layernorm(h, g, b):
    mu = jnp.mean(h, axis=-1, keepdims=True)
    hc = h - mu
    var = jnp.mean(hc * hc, axis=-1, keepdims=True)
    return hc * lax.rsqrt(var + LN_EPS) * g + b


def _mix_kernel(x_ref, oa_ref, ob_ref, wm_ref, bm_ref, wba_ref, wbb_ref, wout_ref, ln1g_ref, ln1b_ref,
                w1_ref, b1_ref, w2_ref, b2_ref, ln2g_ref, ln2b_ref, out_ref):
    x = x_ref[...]
    xb = x.astype(BF16)
    ya = _dot(oa_ref[...], wba_ref[...])
    yb = _dot(ob_ref[...], wbb_ref[...])
    ga = _sigmoid(_dot(xb, wm_ref[:, 0:D_MODEL]) + bm_ref[:, 0:D_MODEL])
    m = ga * ya
    gb = _sigmoid(_dot(xb, wm_ref[:, D_MODEL:]) + bm_ref[:, D_MODEL:])
    m = m + gb * yb
    mix = _dot(m.astype(BF16), wout_ref[...])
    x1 = _layernorm(ALPHA * x + mix, ln1g_ref[...], ln1b_ref[...])
    x1b = x1.astype(BF16)
    acc = jnp.zeros_like(x1)
    for c in range(D_FF // FF_CHUNK):
        cs = slice(c * FF_CHUNK, (c + 1) * FF_CHUNK)
        h = jnp.maximum(_dot(x1b, w1_ref[:, cs]) + b1_ref[:, cs], 0.0)
        acc = acc + _dot((h * h).astype(BF16), w2_ref[cs, :])
    out_ref[...] = _layernorm(ALPHA * x1 + acc + b2_ref[...], ln2g_ref[...], ln2b_ref[...])


def _block_ones(n, blk):
    idx = np.arange(n) // blk
    return jnp.asarray(idx[:, None] == idx[None, :], BF16)


def _layer(x2, bsz, seq, p, tm_proj, tm_mix):
    t = bsz * seq
    row = lambda a: a.reshape(1, -1).astype(F32)

    w_in = p['w_in']
    wg = jnp.pad(w_in[:, :GLA_IN], ((0, 0), (0, GLA_IN_PAD - GLA_IN))).astype(BF16)
    wr = w_in[:, GLA_IN:].astype(BF16)
    wgk = jnp.pad(p['w_gk_up'], ((0, 128 - GLA_GATE_RANK), (0, 0))).astype(BF16)
    wlora = jnp.zeros((LORA_IN, 3 * RWKV_WIDTH), F32)
    wlora = wlora.at[0:64, 0:RWKV_WIDTH].set(p['rwkv_w_up'])
    wlora = wlora.at[64:128, RWKV_WIDTH:2 * RWKV_WIDTH].set(p['rwkv_a_up'])
    wlora = wlora.at[128:256, 2 * RWKV_WIDTH:].set(p['rwkv_g_up']).astype(BF16)
    ones_r = _block_ones(RWKV_WIDTH, RWKV_HEAD)

    n_tiles = t // tm_proj
    tok = lambda w: pl.BlockSpec((tm_proj, w), lambda i: (i, 0))
    bshape = lambda w: jax.ShapeDtypeStruct((t, w), BF16)
    proj_out = pl.pallas_call(
        functools.partial(_proj_kernel, tiles_per_seq=seq // tm_proj),
        grid=(n_tiles,),
        in_specs=[tok(D_MODEL), _const_spec((D_MODEL, GLA_IN_PAD)), _const_spec((D_MODEL, RWKV_IN)),
                  _const_spec((1, RWKV_IN)), _const_spec((128, GLA_QK)), _const_spec((1, GLA_QK)),
                  _const_spec((LORA_IN, 3 * RWKV_WIDTH))] + [_const_spec((1, RWKV_WIDTH))] * 5
                 + [_const_spec((RWKV_WIDTH, RWKV_WIDTH))],
        out_specs=[tok(GLA_QK), tok(GLA_QK), tok(GLA_WIDTH), tok(GLA_WIDTH), tok(GLA_QK)]
                  + [tok(RWKV_WIDTH)] * 8,
        out_shape=[bshape(GLA_QK), bshape(GLA_QK), bshape(GLA_WIDTH), bshape(GLA_WIDTH),
                   jax.ShapeDtypeStruct((t, GLA_QK), F32)]
                  + [bshape(RWKV_WIDTH)] * 5 + [jax.ShapeDtypeStruct((t, RWKV_WIDTH), F32)]
                  + [bshape(RWKV_WIDTH)] * 2,
        scratch_shapes=[pltpu.VMEM((1, RWKV_IN), F32)],
        compiler_params=pltpu.CompilerParams(dimension_semantics=("arbitrary",),
                                             vmem_limit_bytes=VMEM_LIMIT_V7X),
        name="proj_prep",
    )(x2, wg, wr, row(p['mu_shift']), wgk, row(p['b_gk']), wlora, row(p['rwkv_w0']), row(p['rwkv_a0']),
      row(p['rwkv_k_k']), row(p['rwkv_k_a']), row(p['rwkv_r_k']), ones_r)
    gq, gk_, gv, gsg, ggk, rr, rk2, rv, rkk, rkb, rlw, rg, rbonus = proj_out

    def block_tri(n_sub):
        return jnp.asarray(np.kron(np.eye(n_sub), np.tril(np.ones((CHUNK, CHUNK)))), BF16)

    gtok = lambda w: pl.BlockSpec((GLA_BLOCK, w), lambda i: (i, 0))
    n_sub_g = GLA_BLOCK // CHUNK
    rid = np.arange(SUB * GLA_QK)
    cid = SUB * ((rid % GLA_QK) // GLA_DK) + (SUB - 1) - rid // GLA_QK
    e_big = jnp.asarray(cid[:, None] == np.arange(128)[None, :], BF16)
    o_a = pl.pallas_call(
        functools.partial(_gla_kernel, steps_per_seq=seq // GLA_BLOCK, n_sub=n_sub_g),
        grid=(t // GLA_BLOCK,),
        in_specs=[gtok(GLA_QK), gtok(GLA_QK), gtok(GLA_WIDTH), gtok(GLA_WIDTH), gtok(GLA_QK),
                  _const_spec((1, GLA_WIDTH)), _const_spec((GLA_BLOCK, GLA_BLOCK)),
                  _const_spec((SUB * GLA_QK, 128))],
        out_specs=gtok(GLA_WIDTH),
        out_shape=bshape(GLA_WIDTH),
        scratch_shapes=[pltpu.VMEM((GLA_DV, GLA_QK), F32), pltpu.VMEM((GLA_BLOCK, GLA_WIDTH), F32),
                        pltpu.VMEM((SUB + GLA_BLOCK, GLA_QK), F32), pltpu.VMEM((SUB + GLA_BLOCK, GLA_QK), F32)],
        compiler_params=pltpu.CompilerParams(dimension_semantics=("arbitrary",),
                                             vmem_limit_bytes=VMEM_LIMIT_V7X),
        name="gla_chunk",
    )(gq, gk_, gv, gsg, ggk, row(jnp.tile(p['gla_norm_w'], GLA_HEADS)), block_tri(n_sub_g), e_big)

    rtok = lambda w: pl.BlockSpec((RWKV_BLOCK, w), lambda i: (i, 0))
    n_sub = RWKV_BLOCK // CHUNK
    tri_r = block_tri(n_sub)
    o_b = pl.pallas_call(
        functools.partial(_rwkv_kernel, steps_per_seq=seq // RWKV_BLOCK, n_sub=n_sub),
        grid=(t // RWKV_BLOCK,),
        in_specs=[rtok(RWKV_WIDTH)] * 8 + [_const_spec((1, RWKV_WIDTH))] * 2
                 + [_const_spec((RWKV_BLOCK, RWKV_BLOCK)), _const_spec((RWKV_WIDTH, RWKV_WIDTH))],
        out_specs=rtok(RWKV_WIDTH),
        out_shape=bshape(RWKV_WIDTH),
        scratch_shapes=[pltpu.VMEM((RWKV_HEAD, RWKV_WIDTH), F32), pltpu.VMEM((RWKV_BLOCK, RWKV_WIDTH), F32)],
        compiler_params=pltpu.CompilerParams(dimension_semantics=("arbitrary",),
                                             vmem_limit_bytes=VMEM_LIMIT_V7X),
        name="rwkv_chunk",
    )(rr, rk2, rv, rkk, rkb, rlw, rg, rbonus, row(p['rwkv_gn_w']), row(p['rwkv_gn_b']), tri_r, ones_r)

    mtok = lambda w: pl.BlockSpec((tm_mix, w), lambda i: (i, 0))
    out = pl.pallas_call(
        _mix_kernel,
        grid=(t // tm_mix,),
        in_specs=[mtok(D_MODEL), mtok(GLA_WIDTH), mtok(RWKV_WIDTH),
                  _const_spec((D_MODEL, 2 * D_MODEL)), _const_spec((1, 2 * D_MODEL)),
                  _const_spec((GLA_WIDTH, D_MODEL)), _const_spec((RWKV_WIDTH, D_MODEL)),
                  _const_spec((D_MODEL, D_MODEL)), _const_spec((1, D_MODEL)), _const_spec((1, D_MODEL)),
                  _const_spec((D_MODEL, D_FF)), _const_spec((1, D_FF)),
                  _const_spec((D_FF, D_MODEL)), _const_spec((1, D_MODEL)),
                  _const_spec((1, D_MODEL)), _const_spec((1, D_MODEL))],
        out_specs=mtok(D_MODEL),
        out_shape=jax.ShapeDtypeStruct((t, D_MODEL), F32),
        compiler_params=pltpu.CompilerParams(dimension_semantics=("arbitrary",),
                                             vmem_limit_bytes=VMEM_LIMIT_V7X),
        name="merge_mlp",
    )(x2, o_a, o_b, p['w_merge'].astype(BF16), row(p['b_merge']),
      p['w_branch'][0].astype(BF16), p['w_branch'][1].astype(BF16), p['w_out'].astype(BF16),
      row(p['ln1_g']), row(p['ln1_b']), p['w_mlp_up'].astype(BF16), row(p['b_mlp_up']),
      p['w_mlp_down'].astype(BF16), row(p['b_mlp_down']), row(p['ln2_g']), row(p['ln2_b']))
    return out


def kernel(x, w_in, mu_shift, w_gk_up, b_gk, gla_norm_w, rwkv_w0, rwkv_w_up, rwkv_a0, rwkv_a_up,
           rwkv_g_up, rwkv_k_k, rwkv_k_a, rwkv_r_k, rwkv_gn_w, rwkv_gn_b, w_merge, b_merge, w_branch,
           w_out, ln1_g, ln1_b, w_mlp_up, b_mlp_up, w_mlp_down, b_mlp_down, ln2_g, ln2_b):
    bsz, seq, d = x.shape
    assert d == D_MODEL and seq % CHUNK == 0
    tm_proj = 512 if seq % 512 == 0 else CHUNK
    tm_mix = 256 if (bsz * seq) % 256 == 0 else CHUNK
    params = dict(w_in=w_in, mu_shift=mu_shift, w_gk_up=w_gk_up, b_gk=b_gk, gla_norm_w=gla_norm_w,
                  rwkv_w0=rwkv_w0, rwkv_w_up=rwkv_w_up, rwkv_a0=rwkv_a0, rwkv_a_up=rwkv_a_up,
                  rwkv_g_up=rwkv_g_up, rwkv_k_k=rwkv_k_k, rwkv_k_a=rwkv_k_a, rwkv_r_k=rwkv_r_k,
                  rwkv_gn_w=rwkv_gn_w, rwkv_gn_b=rwkv_gn_b, w_merge=w_merge, b_merge=b_merge,
                  w_branch=w_branch, w_out=w_out, ln1_g=ln1_g, ln1_b=ln1_b, w_mlp_up=w_mlp_up,
                  b_mlp_up=b_mlp_up, w_mlp_down=w_mlp_down, b_mlp_down=b_mlp_down, ln2_g=ln2_g, ln2_b=ln2_b)
    x2 = x.reshape(bsz * seq, d)
    for l in range(w_in.shape[0]):
        x2 = _layer(x2, bsz, seq, {n: a[l] for n, a in params.items()}, tm_proj, tm_mix)
    return x2.reshape(bsz, seq, d)
```

```python
import functools

import jax
import jax.numpy as jnp
import numpy as np
from jax import lax
from jax.experimental import pallas as pl
from jax.experimental.pallas import tpu as pltpu

F32 = jnp.float32
BF16 = jnp.bfloat16

D_MODEL = 1024
D_FF = 4 * D_MODEL
DEPTH = 1
CHUNK = 64
SUB = 16
GLA_HEADS, GLA_DK, GLA_DV = 4, 64, 128
GLA_GATE_RANK = 16
GLA_GATE_NORMALIZER = 16.0
GLA_NORM_EPS = 1e-5
RWKV_HEADS, RWKV_HEAD = 8, 64
RWKV_GN_EPS = 64e-5
L2_EPS = 1e-12
LN_EPS = 1e-5
ALPHA = (2.0 * DEPTH) ** 0.25
GLA_QK = GLA_HEADS * GLA_DK
GLA_WIDTH = GLA_HEADS * GLA_DV
RWKV_WIDTH = RWKV_HEADS * RWKV_HEAD
GLA_IN = 2 * GLA_QK + 2 * GLA_WIDTH + GLA_GATE_RANK
GLA_IN_PAD = 1664
RWKV_IN = 3 * RWKV_WIDTH + 64 + 64 + 128
LORA_OFF = 3 * RWKV_WIDTH
LORA_IN = 256
FF_CHUNK = 1024
DENSE_SUB = 256
PROJ_TILE = 1024
MIX_TILE = 512
RWKV_SEQS = 8
GLA_BLOCK = 256
LOG2_E = float(np.log2(np.e))
GLA_SAFE_LOG2 = 96.0
VMEM_LIMIT_V7X = 56 * 1024 * 1024


def _dot(a, b):
    return jnp.dot(a, b, preferred_element_type=F32)


def _dot_nt(a, b):
    return lax.dot_general(a, b, (((1,), (1,)), ((), ())), preferred_element_type=F32)


def _dot_tn(a, b):
    return lax.dot_general(a, b, (((0,), (0,)), ((), ())), preferred_element_type=F32)


def _sigmoid(z):
    return 1.0 / (1.0 + jnp.exp(-z))


def _log_sigmoid(z):
    return jnp.minimum(z, 0.0) - jnp.log(1.0 + jnp.exp(-jnp.abs(z)))


def _split_bf16(a):
    hi = a.astype(BF16)
    lo = (a - hi.astype(F32)).astype(BF16)
    return hi, lo


def _const_spec(shape):
    nd = len(shape)
    return pl.BlockSpec(shape, lambda *_: (0,) * nd, pipeline_mode=pl.Buffered(1))


def _proj_kernel(x_ref, wg_ref, wr_ref, mu_ref, wgk_ref, bgk_ref, wlora_ref, w0_ref, a0_ref,
                 kk_ref, ka_ref, rk_ref, ones_ref,
                 q_ref, k_ref, v_ref, sg_ref, gk_ref,
                 r_ref, rk2_ref, rv_ref, kkn_ref, kb_ref, lw_ref, g_ref, bonus_ref,
                 carry_ref, *, tiles_per_seq):
    i = pl.program_id(0)
    n_sub = x_ref.shape[0] // DENSE_SUB

    @pl.when(i % tiles_per_seq == 0)
    def _():
        carry_ref[...] = jnp.zeros_like(carry_ref)

    def project(j):
        xb = x_ref[j * DENSE_SUB:(j + 1) * DENSE_SUB, :].astype(BF16)
        return _dot(xb, wg_ref[...]), _dot(xb, wr_ref[...])

    def prepare(j, hg, hr, last_row):
        rows = slice(j * DENSE_SUB, (j + 1) * DENSE_SUB)
        q_ref[rows, :] = (hg[:, 0:GLA_QK] * (GLA_DK ** -0.5)).astype(BF16)
        k_ref[rows, :] = hg[:, GLA_QK:2 * GLA_QK].astype(BF16)
        v_ref[rows, :] = hg[:, 2 * GLA_QK:2 * GLA_QK + GLA_WIDTH].astype(BF16)
        g_out = hg[:, 2 * GLA_QK + GLA_WIDTH:2 * GLA_QK + 2 * GLA_WIDTH]
        sg_ref[rows, :] = (g_out * _sigmoid(g_out)).astype(BF16)
        z = _dot(hg[:, 1536:1664].astype(BF16), wgk_ref[...]) + bgk_ref[...]
        gk_ref[rows, :] = _log_sigmoid(z) * (1.0 / GLA_GATE_NORMALIZER)

        row = lax.broadcasted_iota(jnp.int32, (DENSE_SUB, 1), 0)
        prev = jnp.where(row == 0, last_row, pltpu.roll(hr, 1, 0))
        u = hr + (prev - hr) * mu_ref[...]
        r = u[:, 0:RWKV_WIDTH]
        k = u[:, RWKV_WIDTH:2 * RWKV_WIDTH]
        v = u[:, 2 * RWKV_WIDTH:3 * RWKV_WIDTH]
        low = u[:, LORA_OFF:LORA_OFF + LORA_IN]
        col = lax.broadcasted_iota(jnp.int32, (1, LORA_IN), 1)
        low = jnp.where(col < 64, jnp.tanh(low), jnp.where(col < 128, low, _sigmoid(low)))
        lo = _dot(low.astype(BF16), wlora_ref[...])
        lw_ref[rows, :] = -float(np.exp(-0.5)) * _sigmoid(w0_ref[...] + lo[:, 0:RWKV_WIDTH])
        a = _sigmoid(a0_ref[...] + lo[:, RWKV_WIDTH:2 * RWKV_WIDTH])
        g_ref[rows, :] = lo[:, 2 * RWKV_WIDTH:3 * RWKV_WIDTH].astype(BF16)

        ones = ones_ref[...]
        kk = k * kk_ref[...]
        ss = _dot((kk * kk).astype(BF16), ones)
        kkn = kk / jnp.maximum(jnp.sqrt(ss), L2_EPS)
        k2 = k * (1.0 + (a - 1.0) * ka_ref[...])
        bonus = _dot((r * k2 * rk_ref[...]).astype(BF16), ones) * v
        r_ref[rows, :] = r.astype(BF16)
        rk2_ref[rows, :] = k2.astype(BF16)
        rv_ref[rows, :] = v.astype(BF16)
        kkn_ref[rows, :] = kkn.astype(BF16)
        kb_ref[rows, :] = (kkn * a).astype(BF16)
        bonus_ref[rows, :] = bonus.astype(BF16)
        return hr[DENSE_SUB - 1:DENSE_SUB, :]

    last_row = carry_ref[...]
    pending = project(0)
    for j in range(n_sub):
        following = project(j + 1) if j + 1 < n_sub else None
        last_row = prepare(j, *pending, last_row)
        pending = following
    carry_ref[...] = last_row


def _gla_kernel(q_ref, k_ref, v_ref, sg_ref, gk_ref, nw_ref, tri_ref, ebig_ref, o_ref,
                s_ref, sc_ref, kp_ref, bp_ref, *, steps_per_seq, n_sub):
    i = pl.program_id(0)

    @pl.when(i % steps_per_seq == 0)
    def _():
        s_ref[...] = jnp.zeros_like(s_ref)

    blk = n_sub * CHUNK
    heads = range(GLA_HEADS)
    subs = range(n_sub)
    hsl = [slice(h * GLA_DK, (h + 1) * GLA_DK) for h in heads]
    vsl = [slice(h * GLA_DV, (h + 1) * GLA_DV) for h in heads]
    rows = [slice(g * CHUNK, (g + 1) * CHUNK) for g in subs]
    pairs = [(g, h) for g in subs for h in heads]

    g_hi, g_lo = _split_bf16(gk_ref[...])
    tri = tri_ref[...]
    b = (_dot(tri, g_hi) + _dot(tri, g_lo)) * LOG2_E
    q = q_ref[...].astype(F32)
    k = k_ref[...].astype(F32)
    v_all = v_ref[...]

    qe = (q * jnp.exp2(b)).astype(BF16)
    row_c = lax.broadcasted_iota(jnp.int32, (CHUNK, 1), 0)
    col_c = lax.broadcasted_iota(jnp.int32, (1, CHUNK), 1)
    safe = jnp.min(b) >= -GLA_SAFE_LOG2

    @pl.when(safe)
    def _():
        kn = (k * jnp.exp2(-b)).astype(BF16)
        raw = {gh: _dot_nt(qe[rows[gh[0]], hsl[gh[1]]], kn[rows[gh[0]], hsl[gh[1]]]) for gh in pairs}
        sc = {gh: jnp.where(row_c >= col_c, raw[gh], 0.0).astype(BF16) for gh in pairs}
        for g, h in pairs:
            sc_ref[rows[g], vsl[h]] = _dot(sc[g, h], v_all[rows[g], vsl[h]])

    @pl.when(jnp.logical_not(safe))
    def _():
        kp_ref[0:SUB, :] = jnp.zeros((SUB, GLA_QK), F32)
        bp_ref[0:SUB, :] = jnp.zeros((SUB, GLA_QK), F32)
        kp_ref[SUB:SUB + blk, :] = k
        bp_ref[SUB:SUB + blk, :] = b
        rmod = lax.broadcasted_iota(jnp.int32, (blk, 1), 0) % SUB
        xs = [(q * k).astype(BF16)]
        for d in range(1, SUB):
            ks = kp_ref[SUB - d:SUB - d + blk, :]
            bs = bp_ref[SUB - d:SUB - d + blk, :]
            xs.append(jnp.where(rmod >= d, q * ks * jnp.exp2(b - bs), 0.0).astype(BF16))
        s_diag = _dot(jnp.concatenate(xs, axis=1), ebig_ref[...])
        lane = lax.broadcasted_iota(jnp.int32, (1, 128), 1)
        for g in subs:
            bg = b[rows[g], :]
            for h in heads:
                qh, kh, bh = q[rows[g], hsl[h]], k[rows[g], hsl[h]], bg[:, hsl[h]]
                md = jnp.where((lane >= SUB * h) & (lane < SUB * (h + 1)), s_diag[rows[g], :], 0.0)
                sc_rows = [jnp.zeros((SUB, CHUNK), F32)]
                for sb in range(1, CHUNK // SUB):
                    lo_r = sb * SUB
                    ref = bh[lo_r - 1:lo_r, :]
                    qi = qh[lo_r:lo_r + SUB, :] * jnp.exp2(bh[lo_r:lo_r + SUB, :] - ref)
                    ki = jnp.where(row_c < lo_r, kh * jnp.exp2(jnp.minimum(ref - bh, 0.0)), 0.0)
                    sc_rows.append(_dot_nt(qi.astype(BF16), ki.astype(BF16)))
                sc = jnp.concatenate(sc_rows, axis=0) + pltpu.roll(
                    md, 128 - (SUB - 1) - SUB * h, 1, stride=1, stride_axis=0)[:, 0:CHUNK]
                sc_ref[rows[g], vsl[h]] = _dot(sc.astype(BF16), v_all[rows[g], vsl[h]])

    b_last = {g: b[(g + 1) * CHUNK - 1:(g + 1) * CHUNK, :] for g in subs}
    kdec = {g: (k[rows[g], :] * jnp.exp2(b_last[g] - b[rows[g], :])).astype(BF16) for g in subs}
    kv = {(g, h): _dot_tn(v_all[rows[g], vsl[h]], kdec[g][:, hsl[h]]) for g, h in pairs}

    state = [s_ref[:, hsl[h]] for h in heads]
    entering = {}
    for g in subs:
        e_last = jnp.exp2(b_last[g])
        for h in heads:
            entering[g, h] = state[h].astype(BF16)
            state[h] = state[h] * e_last[:, hsl[h]] + kv[g, h]
    for h in heads:
        s_ref[:, hsl[h]] = state[h]
    inter = {(g, h): _dot_nt(qe[rows[g], hsl[h]], entering[g, h]) for g, h in pairs}
    for g in subs:
        outs = []
        for h in heads:
            oh = inter[g, h] + sc_ref[rows[g], vsl[h]]
            ms = jnp.mean(oh * oh, axis=-1, keepdims=True)
            outs.append(oh * lax.rsqrt(ms + GLA_NORM_EPS))
        o = jnp.concatenate(outs, axis=-1) * nw_ref[...]
        o_ref[rows[g], :] = (o * sg_ref[rows[g], :].astype(F32)).astype(BF16)


def _rwkv_kernel(r_ref, k_ref, v_ref, kk_ref, kb_ref, lw_ref, g_ref, bonus_ref, gnw_ref, gnb_ref,
                 tri_ref, ones_ref, o_ref, s_ref, y_ref, *, n_sub):
    @pl.when(pl.program_id(1) == 0)
    def _():
        s_ref[...] = jnp.zeros_like(s_ref)

    def load(ref):
        return ref[...].reshape(n_sub * CHUNK, RWKV_WIDTH)

    n_pair = RWKV_HEADS // 2
    psl = [slice(p * 2 * RWKV_HEAD, (p + 1) * 2 * RWKV_HEAD) for p in range(n_pair)]
    subs = range(n_sub)
    rows = [slice(g * CHUNK, (g + 1) * CHUNK) for g in subs]
    chains = [(g, p) for g in subs for p in range(n_pair)]

    lane_w = lax.broadcasted_iota(jnp.int32, (1, RWKV_WIDTH), 1)
    first_w = (lane_w & RWKV_HEAD) == 0

    def halves(a):
        return jnp.where(first_w, a, 0.0).astype(BF16), jnp.where(first_w, 0.0, a).astype(BF16)

    lane_p = lax.broadcasted_iota(jnp.int32, (CHUNK, 2 * RWKV_HEAD), 1)
    row_p = lax.broadcasted_iota(jnp.int32, (CHUNK, 2 * RWKV_HEAD), 0)
    first_p = lane_p < RWKV_HEAD
    col_p = lane_p & (RWKV_HEAD - 1)
    strict = row_p > col_p
    incl = row_p >= col_p
    eye = (row_p == col_p).astype(F32)

    def bdiag(a):
        return jnp.concatenate([jnp.where(first_p, a, 0.0), jnp.where(first_p, 0.0, a)], axis=0).astype(BF16)

    lw = load(lw_ref)
    l_hi, l_lo = _split_bf16(lw)
    tri = tri_ref[...]
    c = _dot(tri, l_hi) + _dot(tri, l_lo)
    e_c = jnp.exp(c)
    e_nc = jnp.exp(-c)
    r = load(r_ref).astype(F32)
    k = load(k_ref).astype(F32)
    kk = load(kk_ref).astype(F32)
    kb = load(kb_ref).astype(F32)
    v_all = load(v_ref)
    at_f = -kk * jnp.exp(c - lw)
    at_all = at_f.astype(BF16)
    at_1, at_2 = halves(at_f)
    rt_all = (r * e_c).astype(BF16)
    bt_1, bt_2 = halves(kb * e_nc)
    kt_1, kt_2 = halves(k * e_nc)
    v_1, v_2 = halves(v_all.astype(F32))

    def cut(a, ch):
        return a[rows[ch[0]], psl[ch[1]]]

    a_mat = {ch: _dot_nt(jnp.concatenate([cut(at_all, ch), cut(rt_all, ch)], axis=0),
                         jnp.concatenate([cut(bt_1, ch), cut(bt_2, ch), cut(kt_1, ch), cut(kt_2, ch)], axis=0))
             for ch in chains}
    a_ab = {ch: jnp.where(strict, a_mat[ch][0:CHUNK, 0:128], 0.0) for ch in chains}
    a_kr = {ch: jnp.concatenate([jnp.where(strict, a_mat[ch][0:CHUNK, 128:], 0.0),
                                 jnp.where(incl, a_mat[ch][CHUNK:, 128:], 0.0)], axis=0).astype(BF16)
            for ch in chains}
    a_rb = {ch: jnp.where(incl, a_mat[ch][CHUNK:, 0:128], 0.0).astype(BF16) for ch in chains}
    av = {ch: _dot(a_kr[ch], jnp.concatenate([cut(v_1, ch), cut(v_2, ch)], axis=0)) for ch in chains}
    y_in = {ch: av[ch][CHUNK:] for ch in chains}
    t_inv = {ch: eye + a_ab[ch] for ch in chains}
    pw = {ch: _dot(a_ab[ch].astype(BF16), bdiag(a_ab[ch])) for ch in chains}
    for _ in range(4):
        pt = {ch: _dot(jnp.concatenate([pw[ch], t_inv[ch]], axis=0).astype(BF16), bdiag(pw[ch])) for ch in chains}
        pw = {ch: pt[ch][0:CHUNK] for ch in chains}
        t_inv = {ch: t_inv[ch] + pt[ch][CHUNK:] for ch in chains}
    t_inv = {ch: t_inv[ch] + _dot(t_inv[ch].astype(BF16), bdiag(pw[ch])) for ch in chains}
    wu = {ch: _dot(t_inv[ch].astype(BF16),
                   jnp.concatenate([jnp.concatenate([cut(at_1, ch), cut(at_2, ch)], axis=0),
                                    bdiag(av[ch][0:CHUNK])], axis=1)) for ch in chains}
    wr_til = {ch: jnp.concatenate([wu[ch][:, 0:128].astype(BF16), cut(rt_all, ch)], axis=0) for ch in chains}
    u_til = {ch: wu[ch][:, 128:] for ch in chains}

    c_last = {g: c[(g + 1) * CHUNK - 1:(g + 1) * CHUNK, :] for g in subs}
    e_rem = {g: jnp.exp(c_last[g] - c[rows[g], :]) for g in subs}
    bh_all = {g: (kb[rows[g], :] * e_rem[g]).astype(BF16) for g in subs}
    kh_all = {g: (k[rows[g], :] * e_rem[g]).astype(BF16) for g in subs}
    e_last = {g: jnp.exp(c_last[g]) for g in subs}
    state = {(g, p): s_ref[g, :, psl[p]] for g, p in chains}
    ws = {ch: _dot_nt(wr_til[ch], bdiag(state[ch])) for ch in chains}
    u = {ch: ws[ch][0:CHUNK] + u_til[ch] for ch in chains}
    for g, p in chains:
        y_ref[rows[g], psl[p]] = ws[g, p][CHUNK:] + _dot(a_rb[g, p], bdiag(u[g, p])) + y_in[g, p]
    upd = {(g, p): _dot_tn(jnp.concatenate([u[g, p].astype(BF16), v_all[rows[g], psl[p]]], axis=0),
                           jnp.concatenate([bh_all[g][:, psl[p]], kh_all[g][:, psl[p]]], axis=0))
           for g, p in chains}
    for g, p in chains:
        s_ref[g, :, psl[p]] = (state[g, p] * e_last[g][:, psl[p]]
                               + jnp.where(first_p, upd[g, p][0:CHUNK], upd[g, p][CHUNK:]))

    ones = ones_ref[...]
    y = y_ref[...]
    mu = _dot(y.astype(BF16), ones) * (1.0 / RWKV_HEAD)
    yc = y - mu
    var = _dot((yc * yc).astype(BF16), ones) * (1.0 / RWKV_HEAD)
    yn = yc * lax.rsqrt(var + RWKV_GN_EPS) * gnw_ref[...] + gnb_ref[...]
    o = ((yn + load(bonus_ref).astype(F32)) * load(g_ref).astype(F32)).astype(BF16)
    o_ref[...] = o.reshape(n_sub, CHUNK, RWKV_WIDTH)


def _layernorm(h, g, b):
    mu = jnp.mean(h, axis=-1, keepdims=True)
    hc = h - mu
    var = jnp.mean(hc * hc, axis=-1, keepdims=True)
    return hc * lax.rsqrt(var + LN_EPS) * g + b


def _mix_kernel(x_ref, oa_ref, ob_ref, wm_ref, bm_ref, wba_ref, wbb_ref, wout_ref, ln1g_ref, ln1b_ref,
                w1_ref, b1_ref, w2_ref, b2_ref, ln2g_ref, ln2b_ref, out_ref):
    n_sub = x_ref.shape[0] // DENSE_SUB

    def merge(j):
        rows = slice(j * DENSE_SUB, (j + 1) * DENSE_SUB)
        x = x_ref[rows, :]
        xb = x.astype(BF16)
        ya = _dot(oa_ref[rows, :], wba_ref[...])
        yb = _dot(ob_ref[rows, :], wbb_ref[...])
        ga = _sigmoid(_dot(xb, wm_ref[:, 0:D_MODEL]) + bm_ref[:, 0:D_MODEL])
        m = ga * ya
        gb = _sigmoid(_dot(xb, wm_ref[:, D_MODEL:]) + bm_ref[:, D_MODEL:])
        m = m + gb * yb
        return ALPHA * x + _dot(m.astype(BF16), wout_ref[...])

    def mlp(j, z):
        rows = slice(j * DENSE_SUB, (j + 1) * DENSE_SUB)
        x1 = _layernorm(z, ln1g_ref[...], ln1b_ref[...])
        x1b = x1.astype(BF16)
        acc = jnp.zeros_like(x1)
        for c in range(D_FF // FF_CHUNK):
            cs = slice(c * FF_CHUNK, (c + 1) * FF_CHUNK)
            h = jnp.maximum(_dot(x1b, w1_ref[:, cs]) + b1_ref[:, cs], 0.0)
            acc = acc + _dot((h * h).astype(BF16), w2_ref[cs, :])
        out_ref[rows, :] = _layernorm(ALPHA * x1 + acc + b2_ref[...], ln2g_ref[...], ln2b_ref[...])

    pending = merge(0)
    for j in range(n_sub):
        following = merge(j + 1) if j + 1 < n_sub else None
        mlp(j, pending)
        pending = following


def _block_ones(n, blk):
    idx = np.arange(n) // blk
    return jnp.asarray(idx[:, None] == idx[None, :], BF16)


def _layer(x2, bsz, seq, p, tm_proj, tm_mix):
    t = bsz * seq
    row = lambda a: a.reshape(1, -1).astype(F32)

    w_in = p['w_in']
    wg = jnp.pad(w_in[:, :GLA_IN], ((0, 0), (0, GLA_IN_PAD - GLA_IN))).astype(BF16)
    wr = w_in[:, GLA_IN:].astype(BF16)
    wgk = jnp.pad(p['w_gk_up'], ((0, 128 - GLA_GATE_RANK), (0, 0))).astype(BF16)
    wlora = jnp.zeros((LORA_IN, 3 * RWKV_WIDTH), F32)
    wlora = wlora.at[0:64, 0:RWKV_WIDTH].set(p['rwkv_w_up'])
    wlora = wlora.at[64:128, RWKV_WIDTH:2 * RWKV_WIDTH].set(p['rwkv_a_up'])
    wlora = wlora.at[128:256, 2 * RWKV_WIDTH:].set(p['rwkv_g_up']).astype(BF16)
    ones_r = _block_ones(RWKV_WIDTH, RWKV_HEAD)

    n_tiles = t // tm_proj
    tok = lambda w: pl.BlockSpec((tm_proj, w), lambda i: (i, 0))
    bshape = lambda w: jax.ShapeDtypeStruct((t, w), BF16)
    proj_out = pl.pallas_call(
        functools.partial(_proj_kernel, tiles_per_seq=seq // tm_proj),
        grid=(n_tiles,),
        in_specs=[tok(D_MODEL), _const_spec((D_MODEL, GLA_IN_PAD)), _const_spec((D_MODEL, RWKV_IN)),
                  _const_spec((1, RWKV_IN)), _const_spec((128, GLA_QK)), _const_spec((1, GLA_QK)),
                  _const_spec((LORA_IN, 3 * RWKV_WIDTH))] + [_const_spec((1, RWKV_WIDTH))] * 5
                 + [_const_spec((RWKV_WIDTH, RWKV_WIDTH))],
        out_specs=[tok(GLA_QK), tok(GLA_QK), tok(GLA_WIDTH), tok(GLA_WIDTH), tok(GLA_QK)]
                  + [tok(RWKV_WIDTH)] * 8,
        out_shape=[bshape(GLA_QK), bshape(GLA_QK), bshape(GLA_WIDTH), bshape(GLA_WIDTH),
                   jax.ShapeDtypeStruct((t, GLA_QK), F32)]
                  + [bshape(RWKV_WIDTH)] * 5 + [jax.ShapeDtypeStruct((t, RWKV_WIDTH), F32)]
                  + [bshape(RWKV_WIDTH)] * 2,
        scratch_shapes=[pltpu.VMEM((1, RWKV_IN), F32)],
        compiler_params=pltpu.CompilerParams(dimension_semantics=("arbitrary",),
                                             vmem_limit_bytes=VMEM_LIMIT_V7X),
        name="proj_prep",
    )(x2, wg, wr, row(p['mu_shift']), wgk, row(p['b_gk']), wlora, row(p['rwkv_w0']), row(p['rwkv_a0']),
      row(p['rwkv_k_k']), row(p['rwkv_k_a']), row(p['rwkv_r_k']), ones_r)
    gq, gk_, gv, gsg, ggk, rr, rk2, rv, rkk, rkb, rlw, rg, rbonus = proj_out

    def block_tri(n_sub):
        return jnp.asarray(np.kron(np.eye(n_sub), np.tril(np.ones((CHUNK, CHUNK)))), BF16)

    gtok = lambda w: pl.BlockSpec((GLA_BLOCK, w), lambda i: (i, 0))
    n_sub_g = GLA_BLOCK // CHUNK
    rid = np.arange(SUB * GLA_QK)
    cid = SUB * ((rid % GLA_QK) // GLA_DK) + (SUB - 1) - rid // GLA_QK
    e_big = jnp.asarray(cid[:, None] == np.arange(128)[None, :], BF16)
    o_a = pl.pallas_call(
        functools.partial(_gla_kernel, steps_per_seq=seq // GLA_BLOCK, n_sub=n_sub_g),
        grid=(t // GLA_BLOCK,),
        in_specs=[gtok(GLA_QK), gtok(GLA_QK), gtok(GLA_WIDTH), gtok(GLA_WIDTH), gtok(GLA_QK),
                  _const_spec((1, GLA_WIDTH)), _const_spec((GLA_BLOCK, GLA_BLOCK)),
                  _const_spec((SUB * GLA_QK, 128))],
        out_specs=gtok(GLA_WIDTH),
        out_shape=bshape(GLA_WIDTH),
        scratch_shapes=[pltpu.VMEM((GLA_DV, GLA_QK), F32), pltpu.VMEM((GLA_BLOCK, GLA_WIDTH), F32),
                        pltpu.VMEM((SUB + GLA_BLOCK, GLA_QK), F32), pltpu.VMEM((SUB + GLA_BLOCK, GLA_QK), F32)],
        compiler_params=pltpu.CompilerParams(dimension_semantics=("arbitrary",),
                                             vmem_limit_bytes=VMEM_LIMIT_V7X),
        name="gla_chunk",
    )(gq, gk_, gv, gsg, ggk, row(jnp.tile(p['gla_norm_w'], GLA_HEADS)), block_tri(n_sub_g), e_big)

    n_seq = max(n for n in (1, 2, 4, RWKV_SEQS) if bsz % n == 0)
    rspec = pl.BlockSpec((n_seq, CHUNK, RWKV_WIDTH), lambda b, n: (b, n, 0))
    seq3 = lambda a: a.reshape(bsz, seq, RWKV_WIDTH)
    o_b = pl.pallas_call(
        functools.partial(_rwkv_kernel, n_sub=n_seq),
        grid=(bsz // n_seq, seq // CHUNK),
        in_specs=[rspec] * 8 + [_const_spec((1, RWKV_WIDTH))] * 2
                 + [_const_spec((n_seq * CHUNK, n_seq * CHUNK)), _const_spec((RWKV_WIDTH, RWKV_WIDTH))],
        out_specs=rspec,
        out_shape=jax.ShapeDtypeStruct((bsz, seq, RWKV_WIDTH), BF16),
        scratch_shapes=[pltpu.VMEM((n_seq, RWKV_HEAD, RWKV_WIDTH), F32),
                        pltpu.VMEM((n_seq * CHUNK, RWKV_WIDTH), F32)],
        compiler_params=pltpu.CompilerParams(dimension_semantics=("arbitrary", "arbitrary"),
                                             vmem_limit_bytes=VMEM_LIMIT_V7X),
        name="rwkv_chunk",
    )(seq3(rr), seq3(rk2), seq3(rv), seq3(rkk), seq3(rkb), seq3(rlw), seq3(rg), seq3(rbonus),
      row(p['rwkv_gn_w']), row(p['rwkv_gn_b']), block_tri(n_seq), ones_r).reshape(t, RWKV_WIDTH)

    mtok = lambda w: pl.BlockSpec((tm_mix, w), lambda i: (i, 0))
    out = pl.pallas_call(
        _mix_kernel,
        grid=(t // tm_mix,),
        in_specs=[mtok(D_MODEL), mtok(GLA_WIDTH), mtok(RWKV_WIDTH),
                  _const_spec((D_MODEL, 2 * D_MODEL)), _const_spec((1, 2 * D_MODEL)),
                  _const_spec((GLA_WIDTH, D_MODEL)), _const_spec((RWKV_WIDTH, D_MODEL)),
                  _const_spec((D_MODEL, D_MODEL)), _const_spec((1, D_MODEL)), _const_spec((1, D_MODEL)),
                  _const_spec((D_MODEL, D_FF)), _const_spec((1, D_FF)),
                  _const_spec((D_FF, D_MODEL)), _const_spec((1, D_MODEL)),
                  _const_spec((1, D_MODEL)), _const_spec((1, D_MODEL))],
        out_specs=mtok(D_MODEL),
        out_shape=jax.ShapeDtypeStruct((t, D_MODEL), F32),
        compiler_params=pltpu.CompilerParams(dimension_semantics=("arbitrary",),
                                             vmem_limit_bytes=VMEM_LIMIT_V7X),
        name="merge_mlp",
    )(x2, o_a, o_b, p['w_merge'].astype(BF16), row(p['b_merge']),
      p['w_branch'][0].astype(BF16), p['w_branch'][1].astype(BF16), p['w_out'].astype(BF16),
      row(p['ln1_g']), row(p['ln1_b']), p['w_mlp_up'].astype(BF16), row(p['b_mlp_up']),
      p['w_mlp_down'].astype(BF16), row(p['b_mlp_down']), row(p['ln2_g']), row(p['ln2_b']))
    return out


def kernel(x, w_in, mu_shift, w_gk_up, b_gk, gla_norm_w, rwkv_w0, rwkv_w_up, rwkv_a0, rwkv_a_up,
           rwkv_g_up, rwkv_k_k, rwkv_k_a, rwkv_r_k, rwkv_gn_w, rwkv_gn_b, w_merge, b_merge, w_branch,
           w_out, ln1_g, ln1_b, w_mlp_up, b_mlp_up, w_mlp_down, b_mlp_down, ln2_g, ln2_b):
    bsz, seq, d = x.shape
    assert d == D_MODEL and seq % GLA_BLOCK == 0
    tm_proj = PROJ_TILE if seq % PROJ_TILE == 0 else DENSE_SUB
    tm_mix = MIX_TILE if (bsz * seq) % MIX_TILE == 0 else DENSE_SUB
    params = dict(w_in=w_in, mu_shift=mu_shift, w_gk_up=w_gk_up, b_gk=b_gk, gla_norm_w=gla_norm_w,
                  rwkv_w0=rwkv_w0, rwkv_w_up=rwkv_w_up, rwkv_a0=rwkv_a0, rwkv_a_up=rwkv_a_up,
                  rwkv_g_up=rwkv_g_up, rwkv_k_k=rwkv_k_k, rwkv_k_a=rwkv_k_a, rwkv_r_k=rwkv_r_k,
                  rwkv_gn_w=rwkv_gn_w, rwkv_gn_b=rwkv_gn_b, w_merge=w_merge, b_merge=b_merge,
                  w_branch=w_branch, w_out=w_out, ln1_g=ln1_g, ln1_b=ln1_b, w_mlp_up=w_mlp_up,
                  b_mlp_up=b_mlp_up, w_mlp_down=w_mlp_down, b_mlp_down=b_mlp_down, ln2_g=ln2_g, ln2_b=ln2_b)
    x2 = x.reshape(bsz * seq, d)
    for l in range(w_in.shape[0]):
        x2 = _layer(x2, bsz, seq, {n: a[l] for n, a in params.items()}, tm_proj, tm_mix)
    return x2.reshape(bsz, seq, d)
```

```python
import functools

import jax
import jax.numpy as jnp
import numpy as np
from jax import lax
from jax.experimental import pallas as pl
from jax.experimental.pallas import tpu as pltpu

F32 = jnp.float32
BF16 = jnp.bfloat16

D_MODEL = 1024
D_FF = 4 * D_MODEL
DEPTH = 1
CHUNK = 64
SUB = 16
GLA_HEADS, GLA_DK, GLA_DV = 4, 64, 128
GLA_GATE_RANK = 16
GLA_GATE_NORMALIZER = 16.0
GLA_NORM_EPS = 1e-5
RWKV_HEADS, RWKV_HEAD = 8, 64
RWKV_GN_EPS = 64e-5
L2_EPS = 1e-12
LN_EPS = 1e-5
ALPHA = (2.0 * DEPTH) ** 0.25
GLA_QK = GLA_HEADS * GLA_DK
GLA_WIDTH = GLA_HEADS * GLA_DV
RWKV_WIDTH = RWKV_HEADS * RWKV_HEAD
GLA_IN = 2 * GLA_QK + 2 * GLA_WIDTH + GLA_GATE_RANK
GLA_IN_PAD = 1664
RWKV_IN = 3 * RWKV_WIDTH + 64 + 64 + 128
LORA_OFF = 3 * RWKV_WIDTH
LORA_IN = 256
FF_CHUNK = 1024
DENSE_SUB = 256
PROJ_TILE = 1024
MIX_TILE = 512
RWKV_SEQS = 8
GLA_BLOCK = 256
LOG2_E = float(np.log2(np.e))
GLA_SAFE_LOG2 = 96.0
VMEM_LIMIT_V7X = 56 * 1024 * 1024


def _dot(a, b):
    return jnp.dot(a, b, preferred_element_type=F32)


def _dot_nt(a, b):
    return lax.dot_general(a, b, (((1,), (1,)), ((), ())), preferred_element_type=F32)


def _dot_tn(a, b):
    return lax.dot_general(a, b, (((0,), (0,)), ((), ())), preferred_element_type=F32)


def _sigmoid(z):
    return 1.0 / (1.0 + jnp.exp(-z))


def _log_sigmoid(z):
    return jnp.minimum(z, 0.0) - jnp.log(1.0 + jnp.exp(-jnp.abs(z)))


def _split_bf16(a):
    hi = a.astype(BF16)
    lo = (a - hi.astype(F32)).astype(BF16)
    return hi, lo


def _const_spec(shape):
    nd = len(shape)
    return pl.BlockSpec(shape, lambda *_: (0,) * nd, pipeline_mode=pl.Buffered(1))


def _proj_kernel(x_ref, wg_ref, wr_ref, mu_ref, wgk_ref, bgk_ref, wlora_ref, w0_ref, a0_ref,
                 kk_ref, ka_ref, rk_ref, ones_ref,
                 q_ref, k_ref, v_ref, sg_ref, gk_ref,
                 r_ref, rk2_ref, rv_ref, kkn_ref, kb_ref, lw_ref, g_ref, bonus_ref,
                 carry_ref, *, tiles_per_seq):
    i = pl.program_id(0)
    n_sub = x_ref.shape[0] // DENSE_SUB

    @pl.when(i % tiles_per_seq == 0)
    def _():
        carry_ref[...] = jnp.zeros_like(carry_ref)

    def project(j):
        xb = x_ref[j * DENSE_SUB:(j + 1) * DENSE_SUB, :].astype(BF16)
        return _dot(xb, wg_ref[...]), _dot(xb, wr_ref[...])

    def prepare(j, hg, hr, last_row):
        rows = slice(j * DENSE_SUB, (j + 1) * DENSE_SUB)
        q_ref[rows, :] = (hg[:, 0:GLA_QK] * (GLA_DK ** -0.5)).astype(BF16)
        k_ref[rows, :] = hg[:, GLA_QK:2 * GLA_QK].astype(BF16)
        v_ref[rows, :] = hg[:, 2 * GLA_QK:2 * GLA_QK + GLA_WIDTH].astype(BF16)
        g_out = hg[:, 2 * GLA_QK + GLA_WIDTH:2 * GLA_QK + 2 * GLA_WIDTH]
        sg_ref[rows, :] = (g_out * _sigmoid(g_out)).astype(BF16)
        z = _dot(hg[:, 1536:1664].astype(BF16), wgk_ref[...]) + bgk_ref[...]
        gk_ref[rows, :] = _log_sigmoid(z) * (1.0 / GLA_GATE_NORMALIZER)

        row = lax.broadcasted_iota(jnp.int32, (DENSE_SUB, 1), 0)
        prev = jnp.where(row == 0, last_row, pltpu.roll(hr, 1, 0))
        u = hr + (prev - hr) * mu_ref[...]
        r = u[:, 0:RWKV_WIDTH]
        k = u[:, RWKV_WIDTH:2 * RWKV_WIDTH]
        v = u[:, 2 * RWKV_WIDTH:3 * RWKV_WIDTH]
        low = u[:, LORA_OFF:LORA_OFF + LORA_IN]
        col = lax.broadcasted_iota(jnp.int32, (1, LORA_IN), 1)
        low = jnp.where(col < 64, jnp.tanh(low), jnp.where(col < 128, low, _sigmoid(low)))
        lo = _dot(low.astype(BF16), wlora_ref[...])
        lw_ref[rows, :] = -float(np.exp(-0.5)) * _sigmoid(w0_ref[...] + lo[:, 0:RWKV_WIDTH])
        a = _sigmoid(a0_ref[...] + lo[:, RWKV_WIDTH:2 * RWKV_WIDTH])
        g_ref[rows, :] = lo[:, 2 * RWKV_WIDTH:3 * RWKV_WIDTH].astype(BF16)

        ones = ones_ref[...]
        kk = k * kk_ref[...]
        ss = _dot((kk * kk).astype(BF16), ones)
        kkn = kk / jnp.maximum(jnp.sqrt(ss), L2_EPS)
        k2 = k * (1.0 + (a - 1.0) * ka_ref[...])
        bonus = _dot((r * k2 * rk_ref[...]).astype(BF16), ones) * v
        r_ref[rows, :] = r.astype(BF16)
        rk2_ref[rows, :] = k2.astype(BF16)
        rv_ref[rows, :] = v.astype(BF16)
        kkn_ref[rows, :] = kkn.astype(BF16)
        kb_ref[rows, :] = (kkn * a).astype(BF16)
        bonus_ref[rows, :] = bonus.astype(BF16)
        return hr[DENSE_SUB - 1:DENSE_SUB, :]

    last_row = carry_ref[...]
    pending = project(0)
    for j in range(n_sub):
        following = project(j + 1) if j + 1 < n_sub else None
        last_row = prepare(j, *pending, last_row)
        pending = following
    carry_ref[...] = last_row


def _gla_kernel(q_ref, k_ref, v_ref, sg_ref, gk_ref, nw_ref, tri_ref, ebig_ref, o_ref,
                s_ref, kp_ref, bp_ref, *, steps_per_seq, n_sub):
    i = pl.program_id(0)

    @pl.when(i % steps_per_seq == 0)
    def _():
        s_ref[...] = jnp.zeros_like(s_ref)

    blk = n_sub * CHUNK
    heads = range(GLA_HEADS)
    subs = range(n_sub)
    hsl = [slice(h * GLA_DK, (h + 1) * GLA_DK) for h in heads]
    vsl = [slice(h * GLA_DV, (h + 1) * GLA_DV) for h in heads]
    rows = [slice(g * CHUNK, (g + 1) * CHUNK) for g in subs]
    n_pair = GLA_HEADS // 2
    ksl = [slice(p * 2 * GLA_DK, (p + 1) * 2 * GLA_DK) for p in range(n_pair)]
    wsl = [slice(p * 2 * GLA_DV, (p + 1) * 2 * GLA_DV) for p in range(n_pair)]
    chains = [(g, p) for g in subs for p in range(n_pair)]
    lane_k = lax.broadcasted_iota(jnp.int32, (1, GLA_QK), 1)
    lane_v = lax.broadcasted_iota(jnp.int32, (1, GLA_WIDTH), 1)
    first_k = (lane_k & GLA_DK) == 0
    first_v = (lane_v & GLA_DV) == 0
    first_s = lax.broadcasted_iota(jnp.int32, (GLA_DV, 2 * GLA_DK), 1) < GLA_DK
    lane_c = lax.broadcasted_iota(jnp.int32, (CHUNK, 2 * GLA_DK), 1)
    causal = lax.broadcasted_iota(jnp.int32, (CHUNK, 2 * GLA_DK), 0) >= (lane_c & (GLA_DK - 1))

    g_hi, g_lo = _split_bf16(gk_ref[...])
    tri = tri_ref[...]
    b = (_dot(tri, g_hi) + _dot(tri, g_lo)) * LOG2_E
    row_c = lax.broadcasted_iota(jnp.int32, (CHUNK, 1), 0)
    col_c = lax.broadcasted_iota(jnp.int32, (1, CHUNK), 1)

    def finish(k, v_all, qe, score):
        b_last = {g: b[(g + 1) * CHUNK - 1:(g + 1) * CHUNK, :] for g in subs}
        kdec = {g: (k[rows[g], :] * jnp.exp2(b_last[g] - b[rows[g], :])).astype(BF16) for g in subs}
        kv = {(g, p): _dot_tn(v_all[rows[g], wsl[p]], kdec[g][:, ksl[p]]) for g, p in chains}
        state = [s_ref[:, ksl[p]] for p in range(n_pair)]
        entering = {}
        for g in subs:
            e_last = jnp.exp2(b_last[g])
            for p in range(n_pair):
                entering[g, p] = jnp.concatenate([jnp.where(first_s, state[p], 0.0),
                                                  jnp.where(first_s, 0.0, state[p])], axis=0).astype(BF16)
                state[p] = state[p] * e_last[:, ksl[p]] + jnp.where(first_s, kv[g, p][0:GLA_DV], kv[g, p][GLA_DV:])
        inter = {(g, p): _dot_nt(qe[rows[g], ksl[p]], entering[g, p]) for g, p in chains}
        for g in subs:
            outs = []
            for p in range(n_pair):
                op = inter[g, p] + score[g, p]
                for oh in (op[:, 0:GLA_DV], op[:, GLA_DV:]):
                    ms = jnp.mean(oh * oh, axis=-1, keepdims=True)
                    outs.append(oh * lax.rsqrt(ms + GLA_NORM_EPS))
            o = jnp.concatenate(outs, axis=-1) * nw_ref[...]
            o_ref[rows[g], :] = (o * sg_ref[rows[g], :].astype(F32)).astype(BF16)
        return state

    safe = jnp.min(b) >= -GLA_SAFE_LOG2

    q = q_ref[...].astype(F32)
    k = k_ref[...].astype(F32)
    v_all = v_ref[...]
    qe = (q * jnp.exp2(b)).astype(BF16)
    kn = k * jnp.exp2(-b)
    kn_1, kn_2 = jnp.where(first_k, kn, 0.0).astype(BF16), jnp.where(first_k, 0.0, kn).astype(BF16)
    vf = v_all.astype(F32)
    v_1, v_2 = jnp.where(first_v, vf, 0.0).astype(BF16), jnp.where(first_v, 0.0, vf).astype(BF16)
    raw = {(g, p): _dot_nt(qe[rows[g], ksl[p]],
                           jnp.concatenate([kn_1[rows[g], ksl[p]], kn_2[rows[g], ksl[p]]], axis=0))
           for g, p in chains}
    sc_fast = {ch: jnp.where(causal, raw[ch], 0.0).astype(BF16) for ch in chains}
    state_fast = finish(k, v_all, qe, {
        (g, p): _dot(sc_fast[g, p], jnp.concatenate([v_1[rows[g], wsl[p]], v_2[rows[g], wsl[p]]], axis=0))
        for g, p in chains})

    @pl.when(safe)
    def _():
        for p in range(n_pair):
            s_ref[:, ksl[p]] = state_fast[p]

    @pl.when(jnp.logical_not(safe))
    def _():
        kp_ref[0:SUB, :] = jnp.zeros((SUB, GLA_QK), F32)
        bp_ref[0:SUB, :] = jnp.zeros((SUB, GLA_QK), F32)
        kp_ref[SUB:SUB + blk, :] = k
        bp_ref[SUB:SUB + blk, :] = b
        rmod = lax.broadcasted_iota(jnp.int32, (blk, 1), 0) % SUB
        xs = [(q * k).astype(BF16)]
        for d in range(1, SUB):
            ks = kp_ref[SUB - d:SUB - d + blk, :]
            bs = bp_ref[SUB - d:SUB - d + blk, :]
            xs.append(jnp.where(rmod >= d, q * ks * jnp.exp2(b - bs), 0.0).astype(BF16))
        s_diag = _dot(jnp.concatenate(xs, axis=1), ebig_ref[...])
        lane = lax.broadcasted_iota(jnp.int32, (1, 128), 1)
        score = {}
        for g in subs:
            bg = b[rows[g], :]
            for h in heads:
                qh, kh, bh = q[rows[g], hsl[h]], k[rows[g], hsl[h]], bg[:, hsl[h]]
                md = jnp.where((lane >= SUB * h) & (lane < SUB * (h + 1)), s_diag[rows[g], :], 0.0)
                sc_rows = [jnp.zeros((SUB, CHUNK), F32)]
                for sb in range(1, CHUNK // SUB):
                    lo_r = sb * SUB
                    ref = bh[lo_r - 1:lo_r, :]
                    qi = qh[lo_r:lo_r + SUB, :] * jnp.exp2(bh[lo_r:lo_r + SUB, :] - ref)
                    ki = jnp.where(row_c < lo_r, kh * jnp.exp2(jnp.minimum(ref - bh, 0.0)), 0.0)
                    sc_rows.append(_dot_nt(qi.astype(BF16), ki.astype(BF16)))
                sc = jnp.concatenate(sc_rows, axis=0) + pltpu.roll(
                    md, 128 - (SUB - 1) - SUB * h, 1, stride=1, stride_axis=0)[:, 0:CHUNK]
                score[g, h] = _dot(sc.astype(BF16), v_all[rows[g], vsl[h]])
        state_slow = finish(k, v_all, qe, {(g, p): jnp.concatenate([score[g, 2 * p], score[g, 2 * p + 1]], axis=1)
                                          for g, p in chains})
        for p in range(n_pair):
            s_ref[:, ksl[p]] = state_slow[p]


def _rwkv_kernel(r_ref, k_ref, v_ref, kk_ref, kb_ref, lw_ref, g_ref, bonus_ref, gnw_ref, gnb_ref,
                 tri_ref, ones_ref, o_ref, s_ref, y_ref, *, n_sub):
    @pl.when(pl.program_id(1) == 0)
    def _():
        s_ref[...] = jnp.zeros_like(s_ref)

    def load(ref):
        return ref[...].reshape(n_sub * CHUNK, RWKV_WIDTH)

    n_pair = RWKV_HEADS // 2
    psl = [slice(p * 2 * RWKV_HEAD, (p + 1) * 2 * RWKV_HEAD) for p in range(n_pair)]
    subs = range(n_sub)
    rows = [slice(g * CHUNK, (g + 1) * CHUNK) for g in subs]
    chains = [(g, p) for g in subs for p in range(n_pair)]

    lane_w = lax.broadcasted_iota(jnp.int32, (1, RWKV_WIDTH), 1)
    first_w = (lane_w & RWKV_HEAD) == 0

    def halves(a):
        return jnp.where(first_w, a, 0.0).astype(BF16), jnp.where(first_w, 0.0, a).astype(BF16)

    lane_p = lax.broadcasted_iota(jnp.int32, (CHUNK, 2 * RWKV_HEAD), 1)
    row_p = lax.broadcasted_iota(jnp.int32, (CHUNK, 2 * RWKV_HEAD), 0)
    first_p = lane_p < RWKV_HEAD
    col_p = lane_p & (RWKV_HEAD - 1)
    strict = row_p > col_p
    incl = row_p >= col_p
    eye = (row_p == col_p).astype(F32)

    def bdiag(a):
        return jnp.concatenate([jnp.where(first_p, a, 0.0), jnp.where(first_p, 0.0, a)], axis=0).astype(BF16)

    lw = load(lw_ref)
    l_hi, l_lo = _split_bf16(lw)
    tri = tri_ref[...]
    c = _dot(tri, l_hi) + _dot(tri, l_lo)
    e_c = jnp.exp(c)
    e_nc = jnp.exp(-c)
    r = load(r_ref).astype(F32)
    k = load(k_ref).astype(F32)
    kk = load(kk_ref).astype(F32)
    kb = load(kb_ref).astype(F32)
    v_all = load(v_ref)
    at_f = -kk * jnp.exp(c - lw)
    at_all = at_f.astype(BF16)
    at_1, at_2 = halves(at_f)
    rt_all = (r * e_c).astype(BF16)
    bt_1, bt_2 = halves(kb * e_nc)
    kt_1, kt_2 = halves(k * e_nc)
    v_1, v_2 = halves(v_all.astype(F32))

    def cut(a, ch):
        return a[rows[ch[0]], psl[ch[1]]]

    a_mat = {ch: _dot_nt(jnp.concatenate([cut(at_all, ch), cut(rt_all, ch)], axis=0),
                         jnp.concatenate([cut(bt_1, ch), cut(bt_2, ch), cut(kt_1, ch), cut(kt_2, ch)], axis=0))
             for ch in chains}
    a_ab = {ch: jnp.where(strict, a_mat[ch][0:CHUNK, 0:128], 0.0) for ch in chains}
    a_kr = {ch: jnp.concatenate([jnp.where(strict, a_mat[ch][0:CHUNK, 128:], 0.0),
                                 jnp.where(incl, a_mat[ch][CHUNK:, 128:], 0.0)], axis=0).astype(BF16)
            for ch in chains}
    a_rb = {ch: jnp.where(incl, a_mat[ch][CHUNK:, 0:128], 0.0).astype(BF16) for ch in chains}
    av = {ch: _dot(a_kr[ch], jnp.concatenate([cut(v_1, ch), cut(v_2, ch)], axis=0)) for ch in chains}
    y_in = {ch: av[ch][CHUNK:] for ch in chains}
    t_inv = {ch: eye + a_ab[ch] for ch in chains}
    pw = {ch: _dot(a_ab[ch].astype(BF16), bdiag(a_ab[ch])) for ch in chains}
    for _ in range(4):
        pt = {ch: _dot(jnp.concatenate([pw[ch], t_inv[ch]], axis=0).astype(BF16), bdiag(pw[ch])) for ch in chains}
        pw = {ch: pt[ch][0:CHUNK] for ch in chains}
        t_inv = {ch: t_inv[ch] + pt[ch][CHUNK:] for ch in chains}
    t_inv = {ch: t_inv[ch] + _dot(t_inv[ch].astype(BF16), bdiag(pw[ch])) for ch in chains}
    wu = {ch: _dot(t_inv[ch].astype(BF16),
                   jnp.concatenate([jnp.concatenate([cut(at_1, ch), cut(at_2, ch)], axis=0),
                                    bdiag(av[ch][0:CHUNK])], axis=1)) for ch in chains}
    wr_til = {ch: jnp.concatenate([wu[ch][:, 0:128].astype(BF16), cut(rt_all, ch)], axis=0) for ch in chains}
    u_til = {ch: wu[ch][:, 128:] for ch in chains}

    c_last = {g: c[(g + 1) * CHUNK - 1:(g + 1) * CHUNK, :] for g in subs}
    e_rem = {g: jnp.exp(c_last[g] - c[rows[g], :]) for g in subs}
    bh_all = {g: (kb[rows[g], :] * e_rem[g]).astype(BF16) for g in subs}
    kh_all = {g: (k[rows[g], :] * e_rem[g]).astype(BF16) for g in subs}
    e_last = {g: jnp.exp(c_last[g]) for g in subs}
    state = {(g, p): s_ref[g, :, psl[p]] for g, p in chains}
    ws = {ch: _dot_nt(wr_til[ch], bdiag(state[ch])) for ch in chains}
    u = {ch: ws[ch][0:CHUNK] + u_til[ch] for ch in chains}
    for g, p in chains:
        y_ref[rows[g], psl[p]] = ws[g, p][CHUNK:] + _dot(a_rb[g, p], bdiag(u[g, p])) + y_in[g, p]
    upd = {(g, p): _dot_tn(jnp.concatenate([u[g, p].astype(BF16), v_all[rows[g], psl[p]]], axis=0),
                           jnp.concatenate([bh_all[g][:, psl[p]], kh_all[g][:, psl[p]]], axis=0))
           for g, p in chains}
    for g, p in chains:
        s_ref[g, :, psl[p]] = (state[g, p] * e_last[g][:, psl[p]]
                               + jnp.where(first_p, upd[g, p][0:CHUNK], upd[g, p][CHUNK:]))

    ones = ones_ref[...]
    y = y_ref[...]
    mu = _dot(y.astype(BF16), ones) * (1.0 / RWKV_HEAD)
    yc = y - mu
    var = _dot((yc * yc).astype(BF16), ones) * (1.0 / RWKV_HEAD)
    yn = yc * lax.rsqrt(var + RWKV_GN_EPS) * gnw_ref[...] + gnb_ref[...]
    o = ((yn + load(bonus_ref).astype(F32)) * load(g_ref).astype(F32)).astype(BF16)
    o_ref[...] = o.reshape(n_sub, CHUNK, RWKV_WIDTH)


def _layernorm(h, g, b):
    mu = jnp.mean(h, axis=-1, keepdims=True)
    hc = h - mu
    var = jnp.mean(hc * hc, axis=-1, keepdims=True)
    return hc * lax.rsqrt(var + LN_EPS) * g + b


def _mix_kernel(x_ref, oa_ref, ob_ref, wm_ref, bm_ref, wba_ref, wbb_ref, wout_ref, ln1g_ref, ln1b_ref,
                w1_ref, b1_ref, w2_ref, b2_ref, ln2g_ref, ln2b_ref, out_ref):
    n_sub = x_ref.shape[0] // DENSE_SUB

    def merge(j):
        rows = slice(j * DENSE_SUB, (j + 1) * DENSE_SUB)
        x = x_ref[rows, :]
        xb = x.astype(BF16)
        ya = _dot(oa_ref[rows, :], wba_ref[...])
        yb = _dot(ob_ref[rows, :], wbb_ref[...])
        ga = _sigmoid(_dot(xb, wm_ref[:, 0:D_MODEL]) + bm_ref[:, 0:D_MODEL])
        m = ga * ya
        gb = _sigmoid(_dot(xb, wm_ref[:, D_MODEL:]) + bm_ref[:, D_MODEL:])
        m = m + gb * yb
        return ALPHA * x + _dot(m.astype(BF16), wout_ref[...])

    def mlp(j, z):
        rows = slice(j * DENSE_SUB, (j + 1) * DENSE_SUB)
        x1 = _layernorm(z, ln1g_ref[...], ln1b_ref[...])
        x1b = x1.astype(BF16)
        acc = jnp.zeros_like(x1)
        for c in range(D_FF // FF_CHUNK):
            cs = slice(c * FF_CHUNK, (c + 1) * FF_CHUNK)
            h = jnp.maximum(_dot(x1b, w1_ref[:, cs]) + b1_ref[:, cs], 0.0)
            acc = acc + _dot((h * h).astype(BF16), w2_ref[cs, :])
        out_ref[rows, :] = _layernorm(ALPHA * x1 + acc + b2_ref[...], ln2g_ref[...], ln2b_ref[...])

    pending = merge(0)
    for j in range(n_sub):
        following = merge(j + 1) if j + 1 < n_sub else None
        mlp(j, pending)
        pending = following


def _block_ones(n, blk):
    idx = np.arange(n) // blk
    return jnp.asarray(idx[:, None] == idx[None, :], BF16)


def _layer(x2, bsz, seq, p, tm_proj, tm_mix):
    t = bsz * seq
    row = lambda a: a.reshape(1, -1).astype(F32)

    w_in = p['w_in']
    wg = jnp.pad(w_in[:, :GLA_IN], ((0, 0), (0, GLA_IN_PAD - GLA_IN))).astype(BF16)
    wr = w_in[:, GLA_IN:].astype(BF16)
    wgk = jnp.pad(p['w_gk_up'], ((0, 128 - GLA_GATE_RANK), (0, 0))).astype(BF16)
    wlora = jnp.zeros((LORA_IN, 3 * RWKV_WIDTH), F32)
    wlora = wlora.at[0:64, 0:RWKV_WIDTH].set(p['rwkv_w_up'])
    wlora = wlora.at[64:128, RWKV_WIDTH:2 * RWKV_WIDTH].set(p['rwkv_a_up'])
    wlora = wlora.at[128:256, 2 * RWKV_WIDTH:].set(p['rwkv_g_up']).astype(BF16)
    ones_r = _block_ones(RWKV_WIDTH, RWKV_HEAD)

    n_tiles = t // tm_proj
    tok = lambda w: pl.BlockSpec((tm_proj, w), lambda i: (i, 0))
    bshape = lambda w: jax.ShapeDtypeStruct((t, w), BF16)
    proj_out = pl.pallas_call(
        functools.partial(_proj_kernel, tiles_per_seq=seq // tm_proj),
        grid=(n_tiles,),
        in_specs=[tok(D_MODEL), _const_spec((D_MODEL, GLA_IN_PAD)), _const_spec((D_MODEL, RWKV_IN)),
                  _const_spec((1, RWKV_IN)), _const_spec((128, GLA_QK)), _const_spec((1, GLA_QK)),
                  _const_spec((LORA_IN, 3 * RWKV_WIDTH))] + [_const_spec((1, RWKV_WIDTH))] * 5
                 + [_const_spec((RWKV_WIDTH, RWKV_WIDTH))],
        out_specs=[tok(GLA_QK), tok(GLA_QK), tok(GLA_WIDTH), tok(GLA_WIDTH), tok(GLA_QK)]
                  + [tok(RWKV_WIDTH)] * 8,
        out_shape=[bshape(GLA_QK), bshape(GLA_QK), bshape(GLA_WIDTH), bshape(GLA_WIDTH),
                   jax.ShapeDtypeStruct((t, GLA_QK), F32)]
                  + [bshape(RWKV_WIDTH)] * 5 + [jax.ShapeDtypeStruct((t, RWKV_WIDTH), F32)]
                  + [bshape(RWKV_WIDTH)] * 2,
        scratch_shapes=[pltpu.VMEM((1, RWKV_IN), F32)],
        compiler_params=pltpu.CompilerParams(dimension_semantics=("arbitrary",),
                                             vmem_limit_bytes=VMEM_LIMIT_V7X),
        name="proj_prep",
    )(x2, wg, wr, row(p['mu_shift']), wgk, row(p['b_gk']), wlora, row(p['rwkv_w0']), row(p['rwkv_a0']),
      row(p['rwkv_k_k']), row(p['rwkv_k_a']), row(p['rwkv_r_k']), ones_r)
    gq, gk_, gv, gsg, ggk, rr, rk2, rv, rkk, rkb, rlw, rg, rbonus = proj_out

    def block_tri(n_sub):
        return jnp.asarray(np.kron(np.eye(n_sub), np.tril(np.ones((CHUNK, CHUNK)))), BF16)

    gtok = lambda w: pl.BlockSpec((GLA_BLOCK, w), lambda i: (i, 0))
    n_sub_g = GLA_BLOCK // CHUNK
    rid = np.arange(SUB * GLA_QK)
    cid = SUB * ((rid % GLA_QK) // GLA_DK) + (SUB - 1) - rid // GLA_QK
    e_big = jnp.asarray(cid[:, None] == np.arange(128)[None, :], BF16)
    o_a = pl.pallas_call(
        functools.partial(_gla_kernel, steps_per_seq=seq // GLA_BLOCK, n_sub=n_sub_g),
        grid=(t // GLA_BLOCK,),
        in_specs=[gtok(GLA_QK), gtok(GLA_QK), gtok(GLA_WIDTH), gtok(GLA_WIDTH), gtok(GLA_QK),
                  _const_spec((1, GLA_WIDTH)), _const_spec((GLA_BLOCK, GLA_BLOCK)),
                  _const_spec((SUB * GLA_QK, 128))],
        out_specs=gtok(GLA_WIDTH),
        out_shape=bshape(GLA_WIDTH),
        scratch_shapes=[pltpu.VMEM((GLA_DV, GLA_QK), F32),
                        pltpu.VMEM((SUB + GLA_BLOCK, GLA_QK), F32), pltpu.VMEM((SUB + GLA_BLOCK, GLA_QK), F32)],
        compiler_params=pltpu.CompilerParams(dimension_semantics=("arbitrary",),
                                             vmem_limit_bytes=VMEM_LIMIT_V7X),
        name="gla_chunk",
    )(gq, gk_, gv, gsg, ggk, row(jnp.tile(p['gla_norm_w'], GLA_HEADS)), block_tri(n_sub_g), e_big)

    n_seq = max(n for n in (1, 2, 4, RWKV_SEQS) if bsz % n == 0)
    rspec = pl.BlockSpec((n_seq, CHUNK, RWKV_WIDTH), lambda b, n: (b, n, 0))
    seq3 = lambda a: a.reshape(bsz, seq, RWKV_WIDTH)
    o_b = pl.pallas_call(
        functools.partial(_rwkv_kernel, n_sub=n_seq),
        grid=(bsz // n_seq, seq // CHUNK),
        in_specs=[rspec] * 8 + [_const_spec((1, RWKV_WIDTH))] * 2
                 + [_const_spec((n_seq * CHUNK, n_seq * CHUNK)), _const_spec((RWKV_WIDTH, RWKV_WIDTH))],
        out_specs=rspec,
        out_shape=jax.ShapeDtypeStruct((bsz, seq, RWKV_WIDTH), BF16),
        scratch_shapes=[pltpu.VMEM((n_seq, RWKV_HEAD, RWKV_WIDTH), F32),
                        pltpu.VMEM((n_seq * CHUNK, RWKV_WIDTH), F32)],
        compiler_params=pltpu.CompilerParams(dimension_semantics=("arbitrary", "arbitrary"),
                                             vmem_limit_bytes=VMEM_LIMIT_V7X),
        name="rwkv_chunk",
    )(seq3(rr), seq3(rk2), seq3(rv), seq3(rkk), seq3(rkb), seq3(rlw), seq3(rg), seq3(rbonus),
      row(p['rwkv_gn_w']), row(p['rwkv_gn_b']), block_tri(n_seq), ones_r).reshape(t, RWKV_WIDTH)

    mtok = lambda w: pl.BlockSpec((tm_mix, w), lambda i: (i, 0))
    out = pl.pallas_call(
        _mix_kernel,
        grid=(t // tm_mix,),
        in_specs=[mtok(D_MODEL), mtok(GLA_WIDTH), mtok(RWKV_WIDTH),
                  _const_spec((D_MODEL, 2 * D_MODEL)), _const_spec((1, 2 * D_MODEL)),
                  _const_spec((GLA_WIDTH, D_MODEL)), _const_spec((RWKV_WIDTH, D_MODEL)),
                  _const_spec((D_MODEL, D_MODEL)), _const_spec((1, D_MODEL)), _const_spec((1, D_MODEL)),
                  _const_spec((D_MODEL, D_FF)), _const_spec((1, D_FF)),
                  _const_spec((D_FF, D_MODEL)), _const_spec((1, D_MODEL)),
                  _const_spec((1, D_MODEL)), _const_spec((1, D_MODEL))],
        out_specs=mtok(D_MODEL),
        out_shape=jax.ShapeDtypeStruct((t, D_MODEL), F32),
        compiler_params=pltpu.CompilerParams(dimension_semantics=("arbitrary",),
                                             vmem_limit_bytes=VMEM_LIMIT_V7X),
        name="merge_mlp",
    )(x2, o_a, o_b, p['w_merge'].astype(BF16), row(p['b_merge']),
      p['w_branch'][0].astype(BF16), p['w_branch'][1].astype(BF16), p['w_out'].astype(BF16),
      row(p['ln1_g']), row(p['ln1_b']), p['w_mlp_up'].astype(BF16), row(p['b_mlp_up']),
      p['w_mlp_down'].astype(BF16), row(p['b_mlp_down']), row(p['ln2_g']), row(p['ln2_b']))
    return out


def kernel(x, w_in, mu_shift, w_gk_up, b_gk, gla_norm_w, rwkv_w0, rwkv_w_up, rwkv_a0, rwkv_a_up,
           rwkv_g_up, rwkv_k_k, rwkv_k_a, rwkv_r_k, rwkv_gn_w, rwkv_gn_b, w_merge, b_merge, w_branch,
           w_out, ln1_g, ln1_b, w_mlp_up, b_mlp_up, w_mlp_down, b_mlp_down, ln2_g, ln2_b):
    bsz, seq, d = x.shape
    assert d == D_MODEL and seq % GLA_BLOCK == 0
    tm_proj = PROJ_TILE if seq % PROJ_TILE == 0 else DENSE_SUB
    tm_mix = MIX_TILE if (bsz * seq) % MIX_TILE == 0 else DENSE_SUB
    params = dict(w_in=w_in, mu_shift=mu_shift, w_gk_up=w_gk_up, b_gk=b_gk, gla_norm_w=gla_norm_w,
                  rwkv_w0=rwkv_w0, rwkv_w_up=rwkv_w_up, rwkv_a0=rwkv_a0, rwkv_a_up=rwkv_a_up,
                  rwkv_g_up=rwkv_g_up, rwkv_k_k=rwkv_k_k, rwkv_k_a=rwkv_k_a, rwkv_r_k=rwkv_r_k,
                  rwkv_gn_w=rwkv_gn_w, rwkv_gn_b=rwkv_gn_b, w_merge=w_merge, b_merge=b_merge,
                  w_branch=w_branch, w_out=w_out, ln1_g=ln1_g, ln1_b=ln1_b, w_mlp_up=w_mlp_up,
                  b_mlp_up=b_mlp_up, w_mlp_down=w_mlp_down, b_mlp_down=b_mlp_down, ln2_g=ln2_g, ln2_b=ln2_b)
    x2 = x.reshape(bsz * seq, d)
    for l in range(w_in.shape[0]):
        x2 = _layer(x2, bsz, seq, {n: a[l] for n, a in params.items()}, tm_proj, tm_mix)
    return x2.reshape(bsz, seq, d)
```

```python
import functools

import jax
import jax.numpy as jnp
import numpy as np
from jax import lax
from jax.experimental import pallas as pl
from jax.experimental.pallas import tpu as pltpu

F32 = jnp.float32
BF16 = jnp.bfloat16

D_MODEL = 1024
D_FF = 4 * D_MODEL
DEPTH = 1
CHUNK = 64
SUB = 16
GLA_HEADS, GLA_DK, GLA_DV = 4, 64, 128
GLA_GATE_RANK = 16
GLA_GATE_NORMALIZER = 16.0
GLA_NORM_EPS = 1e-5
RWKV_HEADS, RWKV_HEAD = 8, 64
RWKV_GN_EPS = 64e-5
L2_EPS = 1e-12
LN_EPS = 1e-5
ALPHA = (2.0 * DEPTH) ** 0.25
GLA_QK = GLA_HEADS * GLA_DK
GLA_WIDTH = GLA_HEADS * GLA_DV
RWKV_WIDTH = RWKV_HEADS * RWKV_HEAD
GLA_IN = 2 * GLA_QK + 2 * GLA_WIDTH + GLA_GATE_RANK
GLA_IN_PAD = 1664
RWKV_IN = 3 * RWKV_WIDTH + 64 + 64 + 128
LORA_OFF = 3 * RWKV_WIDTH
LORA_IN = 256
FF_CHUNK = 1024
DENSE_SUB = 256
PROJ_TILE = 1024
MIX_TILE = 512
RWKV_SEQS = 8
RWKV_SEQ_GROUP = 4
GLA_BLOCK = 256
LOG2_E = float(np.log2(np.e))
GLA_SAFE_LOG2 = 96.0
VMEM_LIMIT_V7X = 56 * 1024 * 1024


def _dot(a, b):
    return jnp.dot(a, b, preferred_element_type=F32)


def _dot_nt(a, b):
    return lax.dot_general(a, b, (((1,), (1,)), ((), ())), preferred_element_type=F32)


def _dot_tn(a, b):
    return lax.dot_general(a, b, (((0,), (0,)), ((), ())), preferred_element_type=F32)


def _sigmoid(z):
    return 1.0 / (1.0 + jnp.exp(-z))


def _log_sigmoid(z):
    return jnp.minimum(z, 0.0) - jnp.log(1.0 + jnp.exp(-jnp.abs(z)))


def _split_bf16(a):
    hi = a.astype(BF16)
    lo = (a - hi.astype(F32)).astype(BF16)
    return hi, lo


def _const_spec(shape):
    nd = len(shape)
    return pl.BlockSpec(shape, lambda *_: (0,) * nd, pipeline_mode=pl.Buffered(1))


def _proj_kernel(x_ref, wg_ref, wr_ref, mu_ref, wgk_ref, bgk_ref, wlora_ref, w0_ref, a0_ref,
                 kk_ref, ka_ref, rk_ref, ones_ref,
                 q_ref, k_ref, v_ref, sg_ref, gk_ref,
                 r_ref, rk2_ref, rv_ref, kkn_ref, kb_ref, lw_ref, g_ref, bonus_ref,
                 carry_ref, *, tiles_per_seq):
    i = pl.program_id(0)
    n_sub = x_ref.shape[0] // DENSE_SUB

    @pl.when(i % tiles_per_seq == 0)
    def _():
        carry_ref[...] = jnp.zeros_like(carry_ref)

    def project(j):
        xb = x_ref[j * DENSE_SUB:(j + 1) * DENSE_SUB, :].astype(BF16)
        return _dot(xb, wg_ref[...]), _dot(xb, wr_ref[...])

    def prepare(j, hg, hr, last_row):
        rows = slice(j * DENSE_SUB, (j + 1) * DENSE_SUB)
        q_ref[rows, :] = (hg[:, 0:GLA_QK] * (GLA_DK ** -0.5)).astype(BF16)
        k_ref[rows, :] = hg[:, GLA_QK:2 * GLA_QK].astype(BF16)
        v_ref[rows, :] = hg[:, 2 * GLA_QK:2 * GLA_QK + GLA_WIDTH].astype(BF16)
        g_out = hg[:, 2 * GLA_QK + GLA_WIDTH:2 * GLA_QK + 2 * GLA_WIDTH]
        sg_ref[rows, :] = (g_out * _sigmoid(g_out)).astype(BF16)
        z = _dot(hg[:, 1536:1664].astype(BF16), wgk_ref[...]) + bgk_ref[...]
        gk_ref[rows, :] = _log_sigmoid(z) * (1.0 / GLA_GATE_NORMALIZER)

        row = lax.broadcasted_iota(jnp.int32, (DENSE_SUB, 1), 0)
        prev = jnp.where(row == 0, last_row, pltpu.roll(hr, 1, 0))
        u = hr + (prev - hr) * mu_ref[...]
        r = u[:, 0:RWKV_WIDTH]
        k = u[:, RWKV_WIDTH:2 * RWKV_WIDTH]
        v = u[:, 2 * RWKV_WIDTH:3 * RWKV_WIDTH]
        low = u[:, LORA_OFF:LORA_OFF + LORA_IN]
        col = lax.broadcasted_iota(jnp.int32, (1, LORA_IN), 1)
        low = jnp.where(col < 64, jnp.tanh(low), jnp.where(col < 128, low, _sigmoid(low)))
        lo = _dot(low.astype(BF16), wlora_ref[...])
        lw_ref[rows, :] = -float(np.exp(-0.5)) * _sigmoid(w0_ref[...] + lo[:, 0:RWKV_WIDTH])
        a = _sigmoid(a0_ref[...] + lo[:, RWKV_WIDTH:2 * RWKV_WIDTH])
        g_ref[rows, :] = lo[:, 2 * RWKV_WIDTH:3 * RWKV_WIDTH].astype(BF16)

        ones = ones_ref[...]
        kk = k * kk_ref[...]
        ss = _dot((kk * kk).astype(BF16), ones)
        kkn = kk / jnp.maximum(jnp.sqrt(ss), L2_EPS)
        k2 = k * (1.0 + (a - 1.0) * ka_ref[...])
        bonus = _dot((r * k2 * rk_ref[...]).astype(BF16), ones) * v
        r_ref[rows, :] = r.astype(BF16)
        rk2_ref[rows, :] = k2.astype(BF16)
        rv_ref[rows, :] = v.astype(BF16)
        kkn_ref[rows, :] = kkn.astype(BF16)
        kb_ref[rows, :] = (kkn * a).astype(BF16)
        bonus_ref[rows, :] = bonus.astype(BF16)
        return hr[DENSE_SUB - 1:DENSE_SUB, :]

    last_row = carry_ref[...]
    pending = project(0)
    for j in range(n_sub):
        following = project(j + 1) if j + 1 < n_sub else None
        last_row = prepare(j, *pending, last_row)
        pending = following
    carry_ref[...] = last_row


def _gla_kernel(q_ref, k_ref, v_ref, sg_ref, gk_ref, nw_ref, tri_ref, ebig_ref, o_ref,
                s_ref, kp_ref, bp_ref, *, steps_per_seq, n_sub):
    i = pl.program_id(0)

    @pl.when(i % steps_per_seq == 0)
    def _():
        s_ref[...] = jnp.zeros_like(s_ref)

    blk = n_sub * CHUNK
    heads = range(GLA_HEADS)
    subs = range(n_sub)
    hsl = [slice(h * GLA_DK, (h + 1) * GLA_DK) for h in heads]
    vsl = [slice(h * GLA_DV, (h + 1) * GLA_DV) for h in heads]
    rows = [slice(g * CHUNK, (g + 1) * CHUNK) for g in subs]
    n_pair = GLA_HEADS // 2
    ksl = [slice(p * 2 * GLA_DK, (p + 1) * 2 * GLA_DK) for p in range(n_pair)]
    wsl = [slice(p * 2 * GLA_DV, (p + 1) * 2 * GLA_DV) for p in range(n_pair)]
    chains = [(g, p) for g in subs for p in range(n_pair)]
    lane_k = lax.broadcasted_iota(jnp.int32, (1, GLA_QK), 1)
    lane_v = lax.broadcasted_iota(jnp.int32, (1, GLA_WIDTH), 1)
    first_k = (lane_k & GLA_DK) == 0
    first_v = (lane_v & GLA_DV) == 0
    first_s = lax.broadcasted_iota(jnp.int32, (GLA_DV, 2 * GLA_DK), 1) < GLA_DK
    lane_c = lax.broadcasted_iota(jnp.int32, (CHUNK, 2 * GLA_DK), 1)
    causal = lax.broadcasted_iota(jnp.int32, (CHUNK, 2 * GLA_DK), 0) >= (lane_c & (GLA_DK - 1))

    b = _dot(tri_ref[...], jnp.concatenate(_split_bf16(gk_ref[...]), axis=0)) * LOG2_E
    row_c = lax.broadcasted_iota(jnp.int32, (CHUNK, 1), 0)

    def finish(k, v_all, qe, score):
        b_last = {g: b[(g + 1) * CHUNK - 1:(g + 1) * CHUNK, :] for g in subs}
        kdec = {g: (k[rows[g], :] * jnp.exp2(b_last[g] - b[rows[g], :])).astype(BF16) for g in subs}
        kv = {(g, p): _dot_tn(v_all[rows[g], wsl[p]], kdec[g][:, ksl[p]]) for g, p in chains}
        state = [s_ref[:, ksl[p]] for p in range(n_pair)]
        entering = {}
        for g in subs:
            e_last = jnp.exp2(b_last[g])
            for p in range(n_pair):
                entering[g, p] = jnp.concatenate([jnp.where(first_s, state[p], 0.0),
                                                  jnp.where(first_s, 0.0, state[p])], axis=0).astype(BF16)
                state[p] = state[p] * e_last[:, ksl[p]] + jnp.where(first_s, kv[g, p][0:GLA_DV], kv[g, p][GLA_DV:])
        inter = {(g, p): _dot_nt(qe[rows[g], ksl[p]], entering[g, p]) for g, p in chains}
        for g in subs:
            outs = []
            for p in range(n_pair):
                op = inter[g, p] + score[g, p]
                for oh in (op[:, 0:GLA_DV], op[:, GLA_DV:]):
                    ms = jnp.mean(oh * oh, axis=-1, keepdims=True)
                    outs.append(oh * lax.rsqrt(ms + GLA_NORM_EPS))
            o = jnp.concatenate(outs, axis=-1) * nw_ref[...]
            o_ref[rows[g], :] = (o * sg_ref[rows[g], :].astype(F32)).astype(BF16)
        return state

    safe = jnp.min(b) >= -GLA_SAFE_LOG2

    q = q_ref[...].astype(F32)
    k = k_ref[...].astype(F32)
    v_all = v_ref[...]
    qe = (q * jnp.exp2(b)).astype(BF16)
    kn = (k * jnp.exp2(-b)).astype(BF16)
    kn_1, kn_2 = kn * jnp.where(first_k, 1.0, 0.0).astype(BF16), kn * jnp.where(first_k, 0.0, 1.0).astype(BF16)
    v_1, v_2 = v_all * jnp.where(first_v, 1.0, 0.0).astype(BF16), v_all * jnp.where(first_v, 0.0, 1.0).astype(BF16)
    raw = {(g, p): _dot_nt(qe[rows[g], ksl[p]],
                           jnp.concatenate([kn_1[rows[g], ksl[p]], kn_2[rows[g], ksl[p]]], axis=0))
           for g, p in chains}
    sc_fast = {ch: jnp.where(causal, raw[ch], 0.0).astype(BF16) for ch in chains}
    state_fast = finish(k, v_all, qe, {
        (g, p): _dot(sc_fast[g, p], jnp.concatenate([v_1[rows[g], wsl[p]], v_2[rows[g], wsl[p]]], axis=0))
        for g, p in chains})

    @pl.when(safe)
    def _():
        for p in range(n_pair):
            s_ref[:, ksl[p]] = state_fast[p]

    @pl.when(jnp.logical_not(safe))
    def _():
        kp_ref[0:SUB, :] = jnp.zeros((SUB, GLA_QK), F32)
        bp_ref[0:SUB, :] = jnp.zeros((SUB, GLA_QK), F32)
        kp_ref[SUB:SUB + blk, :] = k
        bp_ref[SUB:SUB + blk, :] = b
        rmod = lax.broadcasted_iota(jnp.int32, (blk, 1), 0) % SUB
        xs = [(q * k).astype(BF16)]
        for d in range(1, SUB):
            ks = kp_ref[SUB - d:SUB - d + blk, :]
            bs = bp_ref[SUB - d:SUB - d + blk, :]
            xs.append(jnp.where(rmod >= d, q * ks * jnp.exp2(b - bs), 0.0).astype(BF16))
        s_diag = _dot(jnp.concatenate(xs, axis=1), ebig_ref[...])
        lane = lax.broadcasted_iota(jnp.int32, (1, 128), 1)
        score = {}
        for g in subs:
            bg = b[rows[g], :]
            for h in heads:
                qh, kh, bh = q[rows[g], hsl[h]], k[rows[g], hsl[h]], bg[:, hsl[h]]
                md = jnp.where((lane >= SUB * h) & (lane < SUB * (h + 1)), s_diag[rows[g], :], 0.0)
                sc_rows = [jnp.zeros((SUB, CHUNK), F32)]
                for sb in range(1, CHUNK // SUB):
                    lo_r = sb * SUB
                    ref = bh[lo_r - 1:lo_r, :]
                    qi = qh[lo_r:lo_r + SUB, :] * jnp.exp2(bh[lo_r:lo_r + SUB, :] - ref)
                    ki = jnp.where(row_c < lo_r, kh * jnp.exp2(jnp.minimum(ref - bh, 0.0)), 0.0)
                    sc_rows.append(_dot_nt(qi.astype(BF16), ki.astype(BF16)))
                sc = jnp.concatenate(sc_rows, axis=0) + pltpu.roll(
                    md, 128 - (SUB - 1) - SUB * h, 1, stride=1, stride_axis=0)[:, 0:CHUNK]
                score[g, h] = _dot(sc.astype(BF16), v_all[rows[g], vsl[h]])
        state_slow = finish(k, v_all, qe, {(g, p): jnp.concatenate([score[g, 2 * p], score[g, 2 * p + 1]], axis=1)
                                          for g, p in chains})
        for p in range(n_pair):
            s_ref[:, ksl[p]] = state_slow[p]


def _rwkv_kernel(r_ref, k_ref, v_ref, kk_ref, kb_ref, lw_ref, g_ref, bonus_ref, gnw_ref, gnb_ref,
                 tri_ref, ones_ref, o_ref, s_ref, y_ref, *, n_seq, seq_group):
    @pl.when(pl.program_id(1) == 0)
    def _():
        s_ref[...] = jnp.zeros_like(s_ref)

    n_pair = RWKV_HEADS // 2
    psl = [slice(p * 2 * RWKV_HEAD, (p + 1) * 2 * RWKV_HEAD) for p in range(n_pair)]
    subs = range(seq_group)
    rows = [slice(g * CHUNK, (g + 1) * CHUNK) for g in subs]
    chains = [(g, p) for g in subs for p in range(n_pair)]
    n_rows = seq_group * CHUNK

    lane_w = lax.broadcasted_iota(jnp.int32, (1, RWKV_WIDTH), 1)
    first_w = (lane_w & RWKV_HEAD) == 0

    keep_1 = jnp.where(first_w, 1.0, 0.0).astype(BF16)
    keep_2 = jnp.where(first_w, 0.0, 1.0).astype(BF16)

    def halves(a):
        return a * keep_1, a * keep_2

    lane_p = lax.broadcasted_iota(jnp.int32, (CHUNK, 2 * RWKV_HEAD), 1)
    row_p = lax.broadcasted_iota(jnp.int32, (CHUNK, 2 * RWKV_HEAD), 0)
    first_p = lane_p < RWKV_HEAD
    col_p = lane_p & (RWKV_HEAD - 1)
    strict = row_p > col_p
    incl = row_p >= col_p
    eye = (row_p == col_p).astype(F32)

    def bdiag(a):
        return jnp.concatenate([jnp.where(first_p, a, 0.0), jnp.where(first_p, 0.0, a)], axis=0).astype(BF16)

    def cut(a, ch):
        return a[rows[ch[0]], psl[ch[1]]]

    def prepare(grp):
        def load(ref):
            return ref[grp * seq_group:(grp + 1) * seq_group].reshape(n_rows, RWKV_WIDTH)

        lw = load(lw_ref)
        c = _dot(tri_ref[...], jnp.concatenate(_split_bf16(lw), axis=0))
        e_nc = jnp.exp(-c)
        bt_f = load(kb_ref).astype(F32) * e_nc
        kt_f = load(k_ref).astype(F32) * e_nc
        v_all = load(v_ref)
        at_all = (-load(kk_ref).astype(F32) * jnp.exp(c - lw)).astype(BF16)
        d = dict(v_all=v_all, at_all=at_all, rt_all=(load(r_ref).astype(F32) * jnp.exp(c)).astype(BF16))
        d['at_1'], d['at_2'] = halves(at_all)
        d['bt_1'], d['bt_2'] = halves(bt_f.astype(BF16))
        d['kt_1'], d['kt_2'] = halves(kt_f.astype(BF16))
        d['v_1'], d['v_2'] = halves(v_all)
        d['e_last'] = {g: jnp.exp(c[(g + 1) * CHUNK - 1:(g + 1) * CHUNK, :]) for g in subs}
        d['bh'] = {g: (bt_f[rows[g], :] * d['e_last'][g]).astype(BF16) for g in subs}
        d['kh'] = {g: (kt_f[rows[g], :] * d['e_last'][g]).astype(BF16) for g in subs}
        return d

    def first_level(d):
        return {ch: _dot_nt(jnp.concatenate([cut(d['at_all'], ch), cut(d['rt_all'], ch)], axis=0),
                            jnp.concatenate([cut(d['bt_1'], ch), cut(d['bt_2'], ch),
                                             cut(d['kt_1'], ch), cut(d['kt_2'], ch)], axis=0))
                for ch in chains}

    def run_chains(grp, d, a_mat):
        a_ab = {ch: jnp.where(strict, a_mat[ch][0:CHUNK, 0:128], 0.0) for ch in chains}
        a_kr = {ch: jnp.concatenate([jnp.where(strict, a_mat[ch][0:CHUNK, 128:], 0.0),
                                     jnp.where(incl, a_mat[ch][CHUNK:, 128:], 0.0)], axis=0).astype(BF16)
                for ch in chains}
        a_rb = {ch: jnp.where(incl, a_mat[ch][CHUNK:, 0:128], 0.0).astype(BF16) for ch in chains}
        av = {ch: _dot(a_kr[ch], jnp.concatenate([cut(d['v_1'], ch), cut(d['v_2'], ch)], axis=0)) for ch in chains}
        t_inv = {ch: eye + a_ab[ch] for ch in chains}
        pw = {ch: _dot(a_ab[ch].astype(BF16), bdiag(a_ab[ch])) for ch in chains}
        for _ in range(4):
            pt = {ch: _dot(jnp.concatenate([pw[ch], t_inv[ch]], axis=0).astype(BF16), bdiag(pw[ch]))
                  for ch in chains}
            pw = {ch: pt[ch][0:CHUNK] for ch in chains}
            t_inv = {ch: t_inv[ch] + pt[ch][CHUNK:] for ch in chains}
        t_inv = {ch: t_inv[ch] + _dot(t_inv[ch].astype(BF16), bdiag(pw[ch])) for ch in chains}
        wu = {ch: _dot(t_inv[ch].astype(BF16),
                       jnp.concatenate([jnp.concatenate([cut(d['at_1'], ch), cut(d['at_2'], ch)], axis=0),
                                        bdiag(av[ch][0:CHUNK])], axis=1)) for ch in chains}
        wr_til = {ch: jnp.concatenate([wu[ch][:, 0:128].astype(BF16), cut(d['rt_all'], ch)], axis=0)
                  for ch in chains}
        state = {(g, p): s_ref[grp * seq_group + g, :, psl[p]] for g, p in chains}
        ws = {ch: _dot_nt(wr_til[ch], bdiag(state[ch])) for ch in chains}
        u = {ch: ws[ch][0:CHUNK] + wu[ch][:, 128:] for ch in chains}
        for g, p in chains:
            y_ref[grp * n_rows + g * CHUNK:grp * n_rows + (g + 1) * CHUNK, psl[p]] = (
                ws[g, p][CHUNK:] + _dot(a_rb[g, p], bdiag(u[g, p])) + av[g, p][CHUNK:])
        upd = {(g, p): _dot_tn(jnp.concatenate([u[g, p].astype(BF16), d['v_all'][rows[g], psl[p]]], axis=0),
                               jnp.concatenate([d['bh'][g][:, psl[p]], d['kh'][g][:, psl[p]]], axis=0))
               for g, p in chains}
        for g, p in chains:
            s_ref[grp * seq_group + g, :, psl[p]] = (state[g, p] * d['e_last'][g][:, psl[p]]
                                                     + jnp.where(first_p, upd[g, p][0:CHUNK], upd[g, p][CHUNK:]))

    def normalise(grp):
        ones = ones_ref[...]
        y = y_ref[grp * n_rows:(grp + 1) * n_rows, :]
        mu = _dot(y.astype(BF16), ones) * (1.0 / RWKV_HEAD)
        yc = y - mu
        var = _dot((yc * yc).astype(BF16), ones) * (1.0 / RWKV_HEAD)
        yn = yc * lax.rsqrt(var + RWKV_GN_EPS) * gnw_ref[...] + gnb_ref[...]
        gs = slice(grp * seq_group, (grp + 1) * seq_group)
        o = ((yn + bonus_ref[gs].reshape(n_rows, RWKV_WIDTH).astype(F32))
             * g_ref[gs].reshape(n_rows, RWKV_WIDTH).astype(F32)).astype(BF16)
        o_ref[gs] = o.reshape(seq_group, CHUNK, RWKV_WIDTH)

    n_grp = n_seq // seq_group
    pending = prepare(0)
    for grp in range(n_grp):
        following = prepare(grp + 1) if grp + 1 < n_grp else None
        run_chains(grp, pending, first_level(pending))
        if grp > 0:
            normalise(grp - 1)
        pending = following
    normalise(n_grp - 1)


def _layernorm(h, g, b):
    mu = jnp.mean(h, axis=-1, keepdims=True)
    hc = h - mu
    var = jnp.mean(hc * hc, axis=-1, keepdims=True)
    return hc * lax.rsqrt(var + LN_EPS) * g + b


def _mix_kernel(x_ref, oa_ref, ob_ref, wm_ref, bm_ref, wba_ref, wbb_ref, wout_ref, ln1g_ref, ln1b_ref,
                w1_ref, b1_ref, w2_ref, b2_ref, ln2g_ref, ln2b_ref, out_ref):
    n_sub = x_ref.shape[0] // DENSE_SUB

    def merge(j):
        rows = slice(j * DENSE_SUB, (j + 1) * DENSE_SUB)
        x = x_ref[rows, :]
        xb = x.astype(BF16)
        ya = _dot(oa_ref[rows, :], wba_ref[...])
        yb = _dot(ob_ref[rows, :], wbb_ref[...])
        ga = _sigmoid(_dot(xb, wm_ref[:, 0:D_MODEL]) + bm_ref[:, 0:D_MODEL])
        m = ga * ya
        gb = _sigmoid(_dot(xb, wm_ref[:, D_MODEL:]) + bm_ref[:, D_MODEL:])
        m = m + gb * yb
        return ALPHA * x + _dot(m.astype(BF16), wout_ref[...])

    def mlp(j, z):
        rows = slice(j * DENSE_SUB, (j + 1) * DENSE_SUB)
        x1 = _layernorm(z, ln1g_ref[...], ln1b_ref[...])
        x1b = x1.astype(BF16)
        acc = jnp.zeros_like(x1)
        for c in range(D_FF // FF_CHUNK):
            cs = slice(c * FF_CHUNK, (c + 1) * FF_CHUNK)
            h = jnp.maximum(_dot(x1b, w1_ref[:, cs]) + b1_ref[:, cs], 0.0)
            acc = acc + _dot((h * h).astype(BF16), w2_ref[cs, :])
        out_ref[rows, :] = _layernorm(ALPHA * x1 + acc + b2_ref[...], ln2g_ref[...], ln2b_ref[...])

    pending = merge(0)
    for j in range(n_sub):
        following = merge(j + 1) if j + 1 < n_sub else None
        mlp(j, pending)
        pending = following


def _block_ones(n, blk):
    idx = np.arange(n) // blk
    return jnp.asarray(idx[:, None] == idx[None, :], BF16)


def _layer(x2, bsz, seq, p, tm_proj, tm_mix):
    t = bsz * seq
    row = lambda a: a.reshape(1, -1).astype(F32)

    w_in = p['w_in']
    wg = jnp.pad(w_in[:, :GLA_IN], ((0, 0), (0, GLA_IN_PAD - GLA_IN))).astype(BF16)
    wr = w_in[:, GLA_IN:].astype(BF16)
    wgk = jnp.pad(p['w_gk_up'], ((0, 128 - GLA_GATE_RANK), (0, 0))).astype(BF16)
    wlora = jnp.zeros((LORA_IN, 3 * RWKV_WIDTH), F32)
    wlora = wlora.at[0:64, 0:RWKV_WIDTH].set(p['rwkv_w_up'])
    wlora = wlora.at[64:128, RWKV_WIDTH:2 * RWKV_WIDTH].set(p['rwkv_a_up'])
    wlora = wlora.at[128:256, 2 * RWKV_WIDTH:].set(p['rwkv_g_up']).astype(BF16)
    ones_r = _block_ones(RWKV_WIDTH, RWKV_HEAD)

    n_tiles = t // tm_proj
    tok = lambda w: pl.BlockSpec((tm_proj, w), lambda i: (i, 0))
    bshape = lambda w: jax.ShapeDtypeStruct((t, w), BF16)
    proj_out = pl.pallas_call(
        functools.partial(_proj_kernel, tiles_per_seq=seq // tm_proj),
        grid=(n_tiles,),
        in_specs=[tok(D_MODEL), _const_spec((D_MODEL, GLA_IN_PAD)), _const_spec((D_MODEL, RWKV_IN)),
                  _const_spec((1, RWKV_IN)), _const_spec((128, GLA_QK)), _const_spec((1, GLA_QK)),
                  _const_spec((LORA_IN, 3 * RWKV_WIDTH))] + [_const_spec((1, RWKV_WIDTH))] * 5
                 + [_const_spec((RWKV_WIDTH, RWKV_WIDTH))],
        out_specs=[tok(GLA_QK), tok(GLA_QK), tok(GLA_WIDTH), tok(GLA_WIDTH), tok(GLA_QK)]
                  + [tok(RWKV_WIDTH)] * 8,
        out_shape=[bshape(GLA_QK), bshape(GLA_QK), bshape(GLA_WIDTH), bshape(GLA_WIDTH),
                   jax.ShapeDtypeStruct((t, GLA_QK), F32)]
                  + [bshape(RWKV_WIDTH)] * 5 + [jax.ShapeDtypeStruct((t, RWKV_WIDTH), F32)]
                  + [bshape(RWKV_WIDTH)] * 2,
        scratch_shapes=[pltpu.VMEM((1, RWKV_IN), F32)],
        compiler_params=pltpu.CompilerParams(dimension_semantics=("arbitrary",),
                                             vmem_limit_bytes=VMEM_LIMIT_V7X),
        name="proj_prep",
    )(x2, wg, wr, row(p['mu_shift']), wgk, row(p['b_gk']), wlora, row(p['rwkv_w0']), row(p['rwkv_a0']),
      row(p['rwkv_k_k']), row(p['rwkv_k_a']), row(p['rwkv_r_k']), ones_r)
    gq, gk_, gv, gsg, ggk, rr, rk2, rv, rkk, rkb, rlw, rg, rbonus = proj_out

    def block_tri2(n_sub):
        tri = np.kron(np.eye(n_sub), np.tril(np.ones((CHUNK, CHUNK))))
        return jnp.asarray(np.concatenate([tri, tri], axis=1), BF16)

    gtok = lambda w: pl.BlockSpec((GLA_BLOCK, w), lambda i: (i, 0))
    n_sub_g = GLA_BLOCK // CHUNK
    rid = np.arange(SUB * GLA_QK)
    cid = SUB * ((rid % GLA_QK) // GLA_DK) + (SUB - 1) - rid // GLA_QK
    e_big = jnp.asarray(cid[:, None] == np.arange(128)[None, :], BF16)
    o_a = pl.pallas_call(
        functools.partial(_gla_kernel, steps_per_seq=seq // GLA_BLOCK, n_sub=n_sub_g),
        grid=(t // GLA_BLOCK,),
        in_specs=[gtok(GLA_QK), gtok(GLA_QK), gtok(GLA_WIDTH), gtok(GLA_WIDTH), gtok(GLA_QK),
                  _const_spec((1, GLA_WIDTH)), _const_spec((GLA_BLOCK, 2 * GLA_BLOCK)),
                  _const_spec((SUB * GLA_QK, 128))],
        out_specs=gtok(GLA_WIDTH),
        out_shape=bshape(GLA_WIDTH),
        scratch_shapes=[pltpu.VMEM((GLA_DV, GLA_QK), F32),
                        pltpu.VMEM((SUB + GLA_BLOCK, GLA_QK), F32), pltpu.VMEM((SUB + GLA_BLOCK, GLA_QK), F32)],
        compiler_params=pltpu.CompilerParams(dimension_semantics=("arbitrary",),
                                             vmem_limit_bytes=VMEM_LIMIT_V7X),
        name="gla_chunk",
    )(gq, gk_, gv, gsg, ggk, row(jnp.tile(p['gla_norm_w'], GLA_HEADS)), block_tri2(n_sub_g), e_big)

    n_seq = max(n for n in (1, 2, 4, RWKV_SEQS) if bsz % n == 0)
    seq_group = min(RWKV_SEQ_GROUP, n_seq)
    rspec = pl.BlockSpec((n_seq, CHUNK, RWKV_WIDTH), lambda b, n: (b, n, 0))
    seq3 = lambda a: a.reshape(bsz, seq, RWKV_WIDTH)
    o_b = pl.pallas_call(
        functools.partial(_rwkv_kernel, n_seq=n_seq, seq_group=seq_group),
        grid=(bsz // n_seq, seq // CHUNK),
        in_specs=[rspec] * 8 + [_const_spec((1, RWKV_WIDTH))] * 2
                 + [_const_spec((seq_group * CHUNK, 2 * seq_group * CHUNK)), _const_spec((RWKV_WIDTH, RWKV_WIDTH))],
        out_specs=rspec,
        out_shape=jax.ShapeDtypeStruct((bsz, seq, RWKV_WIDTH), BF16),
        scratch_shapes=[pltpu.VMEM((n_seq, RWKV_HEAD, RWKV_WIDTH), F32),
                        pltpu.VMEM((n_seq * CHUNK, RWKV_WIDTH), F32)],
        compiler_params=pltpu.CompilerParams(dimension_semantics=("arbitrary", "arbitrary"),
                                             vmem_limit_bytes=VMEM_LIMIT_V7X),
        name="rwkv_chunk",
    )(seq3(rr), seq3(rk2), seq3(rv), seq3(rkk), seq3(rkb), seq3(rlw), seq3(rg), seq3(rbonus),
      row(p['rwkv_gn_w']), row(p['rwkv_gn_b']), block_tri2(seq_group), ones_r).reshape(t, RWKV_WIDTH)

    mtok = lambda w: pl.BlockSpec((tm_mix, w), lambda i: (i, 0))
    out = pl.pallas_call(
        _mix_kernel,
        grid=(t // tm_mix,),
        in_specs=[mtok(D_MODEL), mtok(GLA_WIDTH), mtok(RWKV_WIDTH),
                  _const_spec((D_MODEL, 2 * D_MODEL)), _const_spec((1, 2 * D_MODEL)),
                  _const_spec((GLA_WIDTH, D_MODEL)), _const_spec((RWKV_WIDTH, D_MODEL)),
                  _const_spec((D_MODEL, D_MODEL)), _const_spec((1, D_MODEL)), _const_spec((1, D_MODEL)),
                  _const_spec((D_MODEL, D_FF)), _const_spec((1, D_FF)),
                  _const_spec((D_FF, D_MODEL)), _const_spec((1, D_MODEL)),
                  _const_spec((1, D_MODEL)), _const_spec((1, D_MODEL))],
        out_specs=mtok(D_MODEL),
        out_shape=jax.ShapeDtypeStruct((t, D_MODEL), F32),
        compiler_params=pltpu.CompilerParams(dimension_semantics=("arbitrary",),
                                             vmem_limit_bytes=VMEM_LIMIT_V7X),
        name="merge_mlp",
    )(x2, o_a, o_b, p['w_merge'].astype(BF16), row(p['b_merge']),
      p['w_branch'][0].astype(BF16), p['w_branch'][1].astype(BF16), p['w_out'].astype(BF16),
      row(p['ln1_g']), row(p['ln1_b']), p['w_mlp_up'].astype(BF16), row(p['b_mlp_up']),
      p['w_mlp_down'].astype(BF16), row(p['b_mlp_down']), row(p['ln2_g']), row(p['ln2_b']))
    return out


def kernel(x, w_in, mu_shift, w_gk_up, b_gk, gla_norm_w, rwkv_w0, rwkv_w_up, rwkv_a0, rwkv_a_up,
           rwkv_g_up, rwkv_k_k, rwkv_k_a, rwkv_r_k, rwkv_gn_w, rwkv_gn_b, w_merge, b_merge, w_branch,
           w_out, ln1_g, ln1_b, w_mlp_up, b_mlp_up, w_mlp_down, b_mlp_down, ln2_g, ln2_b):
    bsz, seq, d = x.shape
    assert d == D_MODEL and seq % GLA_BLOCK == 0
    tm_proj = PROJ_TILE if seq % PROJ_TILE == 0 else DENSE_SUB
    tm_mix = MIX_TILE if (bsz * seq) % MIX_TILE == 0 else DENSE_SUB
    params = dict(w_in=w_in, mu_shift=mu_shift, w_gk_up=w_gk_up, b_gk=b_gk, gla_norm_w=gla_norm_w,
                  rwkv_w0=rwkv_w0, rwkv_w_up=rwkv_w_up, rwkv_a0=rwkv_a0, rwkv_a_up=rwkv_a_up,
                  rwkv_g_up=rwkv_g_up, rwkv_k_k=rwkv_k_k, rwkv_k_a=rwkv_k_a, rwkv_r_k=rwkv_r_k,
                  rwkv_gn_w=rwkv_gn_w, rwkv_gn_b=rwkv_gn_b, w_merge=w_merge, b_merge=b_merge,
                  w_branch=w_branch, w_out=w_out, ln1_g=ln1_g, ln1_b=ln1_b, w_mlp_up=w_mlp_up,
                  b_mlp_up=b_mlp_up, w_mlp_down=w_mlp_down, b_mlp_down=b_mlp_down, ln2_g=ln2_g, ln2_b=ln2_b)
    x2 = x.reshape(bsz * seq, d)
    for l in range(w_in.shape[0]):
        x2 = _layer(x2, bsz, seq, {n: a[l] for n, a in params.items()}, tm_proj, tm_mix)
    return x2.reshape(bsz, seq, d)
```

```python
import functools

import jax
import jax.numpy as jnp
import numpy as np
from jax import lax
from jax.experimental import pallas as pl
from jax.experimental.pallas import tpu as pltpu

F32 = jnp.float32
BF16 = jnp.bfloat16

D_MODEL = 1024
D_FF = 4 * D_MODEL
DEPTH = 1
CHUNK = 64
SUB = 16
GLA_HEADS, GLA_DK, GLA_DV = 4, 64, 128
GLA_GATE_RANK = 16
GLA_GATE_NORMALIZER = 16.0
GLA_NORM_EPS = 1e-5
RWKV_HEADS, RWKV_HEAD = 8, 64
RWKV_GN_EPS = 64e-5
L2_EPS = 1e-12
LN_EPS = 1e-5
ALPHA = (2.0 * DEPTH) ** 0.25
GLA_QK = GLA_HEADS * GLA_DK
GLA_WIDTH = GLA_HEADS * GLA_DV
RWKV_WIDTH = RWKV_HEADS * RWKV_HEAD
GLA_IN = 2 * GLA_QK + 2 * GLA_WIDTH + GLA_GATE_RANK
GLA_IN_PAD = 1664
RWKV_IN = 3 * RWKV_WIDTH + 64 + 64 + 128
LORA_OFF = 3 * RWKV_WIDTH
LORA_IN = 256
FF_CHUNK = 1024
DENSE_SUB = 256
PROJ_TILE = 1024
MIX_TILE = 1024
MIXER_SEQS = 8
MIXER_SEQ_GROUP = 4
LOG2_E = float(np.log2(np.e))
GLA_SAFE_LOG2 = 96.0
VMEM_LIMIT_V7X = 56 * 1024 * 1024


def _dot(a, b):
    return jnp.dot(a, b, preferred_element_type=F32)


def _dot_nt(a, b):
    return lax.dot_general(a, b, (((1,), (1,)), ((), ())), preferred_element_type=F32)


def _dot_tn(a, b):
    return lax.dot_general(a, b, (((0,), (0,)), ((), ())), preferred_element_type=F32)


def _sigmoid(z):
    return 1.0 / (1.0 + jnp.exp(-z))


def _log_sigmoid(z):
    return jnp.minimum(z, 0.0) - jnp.log(1.0 + jnp.exp(-jnp.abs(z)))


def _split_bf16(a):
    hi = a.astype(BF16)
    lo = (a - hi.astype(F32)).astype(BF16)
    return hi, lo


def _const_spec(shape):
    nd = len(shape)
    return pl.BlockSpec(shape, lambda *_: (0,) * nd, pipeline_mode=pl.Buffered(1))


def _proj_kernel(x_ref, wg_ref, wr_ref, mu_ref, wgk_ref, bgk_ref, wlora_ref, w0_ref, a0_ref,
                 kk_ref, ka_ref, rk_ref, ones_ref,
                 q_ref, k_ref, v_ref, sg_ref, gk_ref,
                 r_ref, rk2_ref, rv_ref, kkn_ref, kb_ref, lw_ref, g_ref, bonus_ref,
                 carry_ref, *, tiles_per_seq):
    i = pl.program_id(0)
    n_sub = x_ref.shape[0] // DENSE_SUB

    @pl.when(i % tiles_per_seq == 0)
    def _():
        carry_ref[...] = jnp.zeros_like(carry_ref)

    def project(j):
        xb = x_ref[j * DENSE_SUB:(j + 1) * DENSE_SUB, :].astype(BF16)
        return _dot(xb, wg_ref[...]), _dot(xb, wr_ref[...])

    def prepare(j, hg, hr, last_row):
        rows = slice(j * DENSE_SUB, (j + 1) * DENSE_SUB)
        q_ref[rows, :] = (hg[:, 0:GLA_QK] * (GLA_DK ** -0.5)).astype(BF16)
        k_ref[rows, :] = hg[:, GLA_QK:2 * GLA_QK].astype(BF16)
        v_ref[rows, :] = hg[:, 2 * GLA_QK:2 * GLA_QK + GLA_WIDTH].astype(BF16)
        g_out = hg[:, 2 * GLA_QK + GLA_WIDTH:2 * GLA_QK + 2 * GLA_WIDTH]
        sg_ref[rows, :] = (g_out * _sigmoid(g_out)).astype(BF16)
        z = _dot(hg[:, 1536:1664].astype(BF16), wgk_ref[...]) + bgk_ref[...]
        gk_ref[rows, :] = _log_sigmoid(z) * (1.0 / GLA_GATE_NORMALIZER)

        row = lax.broadcasted_iota(jnp.int32, (DENSE_SUB, 1), 0)
        prev = jnp.where(row == 0, last_row, pltpu.roll(hr, 1, 0))
        u = hr + (prev - hr) * mu_ref[...]
        r = u[:, 0:RWKV_WIDTH]
        k = u[:, RWKV_WIDTH:2 * RWKV_WIDTH]
        v = u[:, 2 * RWKV_WIDTH:3 * RWKV_WIDTH]
        low = u[:, LORA_OFF:LORA_OFF + LORA_IN]
        col = lax.broadcasted_iota(jnp.int32, (1, LORA_IN), 1)
        low = jnp.where(col < 64, jnp.tanh(low), jnp.where(col < 128, low, _sigmoid(low)))
        lo = _dot(low.astype(BF16), wlora_ref[...])
        lw_ref[rows, :] = -float(np.exp(-0.5)) * _sigmoid(w0_ref[...] + lo[:, 0:RWKV_WIDTH])
        a = _sigmoid(a0_ref[...] + lo[:, RWKV_WIDTH:2 * RWKV_WIDTH])
        g_ref[rows, :] = lo[:, 2 * RWKV_WIDTH:3 * RWKV_WIDTH].astype(BF16)

        ones = ones_ref[...]
        kk = k * kk_ref[...]
        ss = _dot((kk * kk).astype(BF16), ones)
        kkn = kk / jnp.maximum(jnp.sqrt(ss), L2_EPS)
        k2 = k * (1.0 + (a - 1.0) * ka_ref[...])
        bonus = _dot((r * k2 * rk_ref[...]).astype(BF16), ones) * v
        r_ref[rows, :] = r.astype(BF16)
        rk2_ref[rows, :] = k2.astype(BF16)
        rv_ref[rows, :] = v.astype(BF16)
        kkn_ref[rows, :] = kkn.astype(BF16)
        kb_ref[rows, :] = (kkn * a).astype(BF16)
        bonus_ref[rows, :] = bonus.astype(BF16)
        return hr[DENSE_SUB - 1:DENSE_SUB, :]

    last_row = carry_ref[...]
    pending = project(0)
    for j in range(n_sub):
        following = project(j + 1) if j + 1 < n_sub else None
        last_row = prepare(j, *pending, last_row)
        pending = following
    carry_ref[...] = last_row


def _gla_part(q_ref, k_ref, v_ref, sg_ref, gk_ref, nw_ref, tri_ref, ebig_ref, o_ref,
              s_ref, kp_ref, bp_ref, *, n_seq, seq_group):
    blk = n_seq * CHUNK
    heads = range(GLA_HEADS)
    subs = range(n_seq)
    hsl = [slice(h * GLA_DK, (h + 1) * GLA_DK) for h in heads]
    vsl = [slice(h * GLA_DV, (h + 1) * GLA_DV) for h in heads]
    rows = [slice(g * CHUNK, (g + 1) * CHUNK) for g in subs]
    n_pair = GLA_HEADS // 2
    ksl = [slice(p * 2 * GLA_DK, (p + 1) * 2 * GLA_DK) for p in range(n_pair)]
    wsl = [slice(p * 2 * GLA_DV, (p + 1) * 2 * GLA_DV) for p in range(n_pair)]
    chains = [(g, p) for g in subs for p in range(n_pair)]
    lane_k = lax.broadcasted_iota(jnp.int32, (1, GLA_QK), 1)
    lane_v = lax.broadcasted_iota(jnp.int32, (1, GLA_WIDTH), 1)
    first_k = (lane_k & GLA_DK) == 0
    first_v = (lane_v & GLA_DV) == 0
    first_s = lax.broadcasted_iota(jnp.int32, (GLA_DV, 2 * GLA_DK), 1) < GLA_DK
    lane_c = lax.broadcasted_iota(jnp.int32, (CHUNK, 2 * GLA_DK), 1)
    causal = lax.broadcasted_iota(jnp.int32, (CHUNK, 2 * GLA_DK), 0) >= (lane_c & (GLA_DK - 1))
    row_c = lax.broadcasted_iota(jnp.int32, (CHUNK, 1), 0)

    def load(ref):
        return ref[...].reshape(blk, ref.shape[-1])

    gk = load(gk_ref)
    n_rows = seq_group * CHUNK
    b = jnp.concatenate([_dot(tri_ref[...], jnp.concatenate(_split_bf16(gk[j * n_rows:(j + 1) * n_rows, :]), axis=0))
                         for j in range(n_seq // seq_group)], axis=0) * LOG2_E

    def finish(k, v_all, qe, score):
        b_last = {g: b[(g + 1) * CHUNK - 1:(g + 1) * CHUNK, :] for g in subs}
        kdec = {g: (k[rows[g], :] * jnp.exp2(b_last[g] - b[rows[g], :])).astype(BF16) for g in subs}
        kv = {(g, p): _dot_tn(v_all[rows[g], wsl[p]], kdec[g][:, ksl[p]]) for g, p in chains}
        state = {(g, p): s_ref[g, :, ksl[p]] for g, p in chains}
        entering = {ch: jnp.concatenate([jnp.where(first_s, state[ch], 0.0),
                                         jnp.where(first_s, 0.0, state[ch])], axis=0).astype(BF16) for ch in chains}
        new_state = {(g, p): state[g, p] * jnp.exp2(b_last[g])[:, ksl[p]]
                     + jnp.where(first_s, kv[g, p][0:GLA_DV], kv[g, p][GLA_DV:]) for g, p in chains}
        inter = {(g, p): _dot_nt(qe[rows[g], ksl[p]], entering[g, p]) for g, p in chains}
        sg = load(sg_ref)
        outs = []
        for g in subs:
            pieces = []
            for p in range(n_pair):
                op = inter[g, p] + score[g, p]
                for oh in (op[:, 0:GLA_DV], op[:, GLA_DV:]):
                    ms = jnp.mean(oh * oh, axis=-1, keepdims=True)
                    pieces.append(oh * lax.rsqrt(ms + GLA_NORM_EPS))
            o = jnp.concatenate(pieces, axis=-1) * nw_ref[...]
            outs.append((o * sg[rows[g], :].astype(F32)).astype(BF16))
        o_ref[...] = jnp.stack(outs, axis=0)
        return new_state

    safe = jnp.min(b) >= -GLA_SAFE_LOG2

    q = load(q_ref).astype(F32)
    k = load(k_ref).astype(F32)
    v_all = load(v_ref)
    qe = (q * jnp.exp2(b)).astype(BF16)
    kn = (k * jnp.exp2(-b)).astype(BF16)
    kn_1, kn_2 = kn * jnp.where(first_k, 1.0, 0.0).astype(BF16), kn * jnp.where(first_k, 0.0, 1.0).astype(BF16)
    v_1, v_2 = v_all * jnp.where(first_v, 1.0, 0.0).astype(BF16), v_all * jnp.where(first_v, 0.0, 1.0).astype(BF16)
    raw = {(g, p): _dot_nt(qe[rows[g], ksl[p]],
                           jnp.concatenate([kn_1[rows[g], ksl[p]], kn_2[rows[g], ksl[p]]], axis=0))
           for g, p in chains}
    sc_fast = {ch: jnp.where(causal, raw[ch], 0.0).astype(BF16) for ch in chains}
    state_fast = finish(k, v_all, qe, {
        (g, p): _dot(sc_fast[g, p], jnp.concatenate([v_1[rows[g], wsl[p]], v_2[rows[g], wsl[p]]], axis=0))
        for g, p in chains})

    @pl.when(safe)
    def _():
        for g, p in chains:
            s_ref[g, :, ksl[p]] = state_fast[g, p]

    @pl.when(jnp.logical_not(safe))
    def _():
        kp_ref[0:SUB, :] = jnp.zeros((SUB, GLA_QK), F32)
        bp_ref[0:SUB, :] = jnp.zeros((SUB, GLA_QK), F32)
        kp_ref[SUB:SUB + blk, :] = k
        bp_ref[SUB:SUB + blk, :] = b
        rmod = lax.broadcasted_iota(jnp.int32, (blk, 1), 0) % SUB
        xs = [(q * k).astype(BF16)]
        for d in range(1, SUB):
            ks = kp_ref[SUB - d:SUB - d + blk, :]
            bs = bp_ref[SUB - d:SUB - d + blk, :]
            xs.append(jnp.where(rmod >= d, q * ks * jnp.exp2(b - bs), 0.0).astype(BF16))
        s_diag = _dot(jnp.concatenate(xs, axis=1), ebig_ref[...])
        lane = lax.broadcasted_iota(jnp.int32, (1, 128), 1)
        score = {}
        for g in subs:
            bg = b[rows[g], :]
            for h in heads:
                qh, kh, bh = q[rows[g], hsl[h]], k[rows[g], hsl[h]], bg[:, hsl[h]]
                md = jnp.where((lane >= SUB * h) & (lane < SUB * (h + 1)), s_diag[rows[g], :], 0.0)
                sc_rows = [jnp.zeros((SUB, CHUNK), F32)]
                for sb in range(1, CHUNK // SUB):
                    lo_r = sb * SUB
                    ref = bh[lo_r - 1:lo_r, :]
                    qi = qh[lo_r:lo_r + SUB, :] * jnp.exp2(bh[lo_r:lo_r + SUB, :] - ref)
                    ki = jnp.where(row_c < lo_r, kh * jnp.exp2(jnp.minimum(ref - bh, 0.0)), 0.0)
                    sc_rows.append(_dot_nt(qi.astype(BF16), ki.astype(BF16)))
                sc = jnp.concatenate(sc_rows, axis=0) + pltpu.roll(
                    md, 128 - (SUB - 1) - SUB * h, 1, stride=1, stride_axis=0)[:, 0:CHUNK]
                score[g, h] = _dot(sc.astype(BF16), v_all[rows[g], vsl[h]])
        state_slow = finish(k, v_all, qe, {(g, p): jnp.concatenate([score[g, 2 * p], score[g, 2 * p + 1]], axis=1)
                                          for g, p in chains})
        for g, p in chains:
            s_ref[g, :, ksl[p]] = state_slow[g, p]


def _rwkv_part(r_ref, k_ref, v_ref, kk_ref, kb_ref, lw_ref, g_ref, bonus_ref, gnw_ref, gnb_ref,
               tri_ref, ones_ref, o_ref, s_ref, y_ref, *, n_seq, seq_group):
    n_pair = RWKV_HEADS // 2
    psl = [slice(p * 2 * RWKV_HEAD, (p + 1) * 2 * RWKV_HEAD) for p in range(n_pair)]
    subs = range(seq_group)
    rows = [slice(g * CHUNK, (g + 1) * CHUNK) for g in subs]
    chains = [(g, p) for g in subs for p in range(n_pair)]
    n_rows = seq_group * CHUNK

    lane_w = lax.broadcasted_iota(jnp.int32, (1, RWKV_WIDTH), 1)
    first_w = (lane_w & RWKV_HEAD) == 0

    keep_1 = jnp.where(first_w, 1.0, 0.0).astype(BF16)
    keep_2 = jnp.where(first_w, 0.0, 1.0).astype(BF16)

    def halves(a):
        return a * keep_1, a * keep_2

    lane_p = lax.broadcasted_iota(jnp.int32, (CHUNK, 2 * RWKV_HEAD), 1)
    row_p = lax.broadcasted_iota(jnp.int32, (CHUNK, 2 * RWKV_HEAD), 0)
    first_p = lane_p < RWKV_HEAD
    col_p = lane_p & (RWKV_HEAD - 1)
    strict = row_p > col_p
    incl = row_p >= col_p
    eye = (row_p == col_p).astype(F32)

    def bdiag(a):
        return jnp.concatenate([jnp.where(first_p, a, 0.0), jnp.where(first_p, 0.0, a)], axis=0).astype(BF16)

    def cut(a, ch):
        return a[rows[ch[0]], psl[ch[1]]]

    def prepare(grp):
        def load(ref):
            return ref[grp * seq_group:(grp + 1) * seq_group].reshape(n_rows, RWKV_WIDTH)

        lw = load(lw_ref)
        c = _dot(tri_ref[...], jnp.concatenate(_split_bf16(lw), axis=0))
        e_nc = jnp.exp(-c)
        bt_f = load(kb_ref).astype(F32) * e_nc
        kt_f = load(k_ref).astype(F32) * e_nc
        v_all = load(v_ref)
        at_all = (-load(kk_ref).astype(F32) * jnp.exp(c - lw)).astype(BF16)
        d = dict(v_all=v_all, at_all=at_all, rt_all=(load(r_ref).astype(F32) * jnp.exp(c)).astype(BF16))
        d['at_1'], d['at_2'] = halves(at_all)
        d['bt_1'], d['bt_2'] = halves(bt_f.astype(BF16))
        d['kt_1'], d['kt_2'] = halves(kt_f.astype(BF16))
        d['v_1'], d['v_2'] = halves(v_all)
        d['e_last'] = {g: jnp.exp(c[(g + 1) * CHUNK - 1:(g + 1) * CHUNK, :]) for g in subs}
        d['bh'] = {g: (bt_f[rows[g], :] * d['e_last'][g]).astype(BF16) for g in subs}
        d['kh'] = {g: (kt_f[rows[g], :] * d['e_last'][g]).astype(BF16) for g in subs}
        return d

    def first_level(d):
        return {ch: _dot_nt(jnp.concatenate([cut(d['at_all'], ch), cut(d['rt_all'], ch)], axis=0),
                            jnp.concatenate([cut(d['bt_1'], ch), cut(d['bt_2'], ch),
                                             cut(d['kt_1'], ch), cut(d['kt_2'], ch)], axis=0))
                for ch in chains}

    def run_chains(grp, d, a_mat):
        a_ab = {ch: jnp.where(strict, a_mat[ch][0:CHUNK, 0:128], 0.0) for ch in chains}
        a_kr = {ch: jnp.concatenate([jnp.where(strict, a_mat[ch][0:CHUNK, 128:], 0.0),
                                     jnp.where(incl, a_mat[ch][CHUNK:, 128:], 0.0)], axis=0).astype(BF16)
                for ch in chains}
        a_rb = {ch: jnp.where(incl, a_mat[ch][CHUNK:, 0:128], 0.0).astype(BF16) for ch in chains}
        av = {ch: _dot(a_kr[ch], jnp.concatenate([cut(d['v_1'], ch), cut(d['v_2'], ch)], axis=0)) for ch in chains}
        t_inv = {ch: eye + a_ab[ch] for ch in chains}
        pw = {ch: _dot(a_ab[ch].astype(BF16), bdiag(a_ab[ch])) for ch in chains}
        for _ in range(4):
            pt = {ch: _dot(jnp.concatenate([pw[ch], t_inv[ch]], axis=0).astype(BF16), bdiag(pw[ch]))
                  for ch in chains}
            pw = {ch: pt[ch][0:CHUNK] for ch in chains}
            t_inv = {ch: t_inv[ch] + pt[ch][CHUNK:] for ch in chains}
        t_inv = {ch: t_inv[ch] + _dot(t_inv[ch].astype(BF16), bdiag(pw[ch])) for ch in chains}
        wu = {ch: _dot(t_inv[ch].astype(BF16),
                       jnp.concatenate([jnp.concatenate([cut(d['at_1'], ch), cut(d['at_2'], ch)], axis=0),
                                        bdiag(av[ch][0:CHUNK])], axis=1)) for ch in chains}
        wr_til = {ch: jnp.concatenate([wu[ch][:, 0:128].astype(BF16), cut(d['rt_all'], ch)], axis=0)
                  for ch in chains}
        state = {(g, p): s_ref[grp * seq_group + g, :, psl[p]] for g, p in chains}
        ws = {ch: _dot_nt(wr_til[ch], bdiag(state[ch])) for ch in chains}
        u = {ch: ws[ch][0:CHUNK] + wu[ch][:, 128:] for ch in chains}
        for g, p in chains:
            y_ref[grp * n_rows + g * CHUNK:grp * n_rows + (g + 1) * CHUNK, psl[p]] = (
                ws[g, p][CHUNK:] + _dot(a_rb[g, p], bdiag(u[g, p])) + av[g, p][CHUNK:])
        upd = {(g, p): _dot_tn(jnp.concatenate([u[g, p].astype(BF16), d['v_all'][rows[g], psl[p]]], axis=0),
                               jnp.concatenate([d['bh'][g][:, psl[p]], d['kh'][g][:, psl[p]]], axis=0))
               for g, p in chains}
        for g, p in chains:
            s_ref[grp * seq_group + g, :, psl[p]] = (state[g, p] * d['e_last'][g][:, psl[p]]
                                                     + jnp.where(first_p, upd[g, p][0:CHUNK], upd[g, p][CHUNK:]))

    def normalise(grp):
        ones = ones_ref[...]
        y = y_ref[grp * n_rows:(grp + 1) * n_rows, :]
        mu = _dot(y.astype(BF16), ones) * (1.0 / RWKV_HEAD)
        yc = y - mu
        var = _dot((yc * yc).astype(BF16), ones) * (1.0 / RWKV_HEAD)
        yn = yc * lax.rsqrt(var + RWKV_GN_EPS) * gnw_ref[...] + gnb_ref[...]
        gs = slice(grp * seq_group, (grp + 1) * seq_group)
        o = ((yn + bonus_ref[gs].reshape(n_rows, RWKV_WIDTH).astype(F32))
             * g_ref[gs].reshape(n_rows, RWKV_WIDTH).astype(F32)).astype(BF16)
        o_ref[gs] = o.reshape(seq_group, CHUNK, RWKV_WIDTH)

    n_grp = n_seq // seq_group
    pending = prepare(0)
    for grp in range(n_grp):
        following = prepare(grp + 1) if grp + 1 < n_grp else None
        run_chains(grp, pending, first_level(pending))
        if grp > 0:
            normalise(grp - 1)
        pending = following
    normalise(n_grp - 1)


def _mixers_kernel(gq_ref, gk_ref, gv_ref, gsg_ref, ggk_ref, gnw_ref, ebig_ref,
                   r_ref, k_ref, v_ref, kk_ref, kb_ref, lw_ref, g_ref, bonus_ref, rnw_ref, rnb_ref,
                   tri_ref, ones_ref, oa_ref, ob_ref,
                   gs_ref, kp_ref, bp_ref, rs_ref, y_ref, *, n_seq, seq_group):
    @pl.when(pl.program_id(1) == 0)
    def _():
        gs_ref[...] = jnp.zeros_like(gs_ref)
        rs_ref[...] = jnp.zeros_like(rs_ref)

    _rwkv_part(r_ref, k_ref, v_ref, kk_ref, kb_ref, lw_ref, g_ref, bonus_ref, rnw_ref, rnb_ref,
               tri_ref, ones_ref, ob_ref, rs_ref, y_ref, n_seq=n_seq, seq_group=seq_group)
    _gla_part(gq_ref, gk_ref, gv_ref, gsg_ref, ggk_ref, gnw_ref, tri_ref, ebig_ref, oa_ref,
              gs_ref, kp_ref, bp_ref, n_seq=n_seq, seq_group=seq_group)


def _layernorm(h, g, b):
    mu = jnp.mean(h, axis=-1, keepdims=True)
    hc = h - mu
    var = jnp.mean(hc * hc, axis=-1, keepdims=True)
    return hc * lax.rsqrt(var + LN_EPS) * g + b


def _mix_kernel(x_ref, oa_ref, ob_ref, wm_ref, bm_ref, wba_ref, wbb_ref, wout_ref, ln1g_ref, ln1b_ref,
                w1_ref, b1_ref, w2_ref, b2_ref, ln2g_ref, ln2b_ref, out_ref):
    n_sub = x_ref.shape[0] // DENSE_SUB

    def merge(j):
        rows = slice(j * DENSE_SUB, (j + 1) * DENSE_SUB)
        x = x_ref[rows, :]
        xb = x.astype(BF16)
        ya = _dot(oa_ref[rows, :], wba_ref[...])
        yb = _dot(ob_ref[rows, :], wbb_ref[...])
        ga = _sigmoid(_dot(xb, wm_ref[:, 0:D_MODEL]) + bm_ref[:, 0:D_MODEL])
        m = ga * ya
        gb = _sigmoid(_dot(xb, wm_ref[:, D_MODEL:]) + bm_ref[:, D_MODEL:])
        m = m + gb * yb
        return ALPHA * x + _dot(m.astype(BF16), wout_ref[...])

    def mlp(j, z):
        rows = slice(j * DENSE_SUB, (j + 1) * DENSE_SUB)
        x1 = _layernorm(z, ln1g_ref[...], ln1b_ref[...])
        x1b = x1.astype(BF16)
        acc = jnp.zeros_like(x1)
        for c in range(D_FF // FF_CHUNK):
            cs = slice(c * FF_CHUNK, (c + 1) * FF_CHUNK)
            h = jnp.maximum(_dot(x1b, w1_ref[:, cs]) + b1_ref[:, cs], 0.0)
            acc = acc + _dot((h * h).astype(BF16), w2_ref[cs, :])
        out_ref[rows, :] = _layernorm(ALPHA * x1 + acc + b2_ref[...], ln2g_ref[...], ln2b_ref[...])

    pending = merge(0)
    for j in range(n_sub):
        following = merge(j + 1) if j + 1 < n_sub else None
        mlp(j, pending)
        pending = following


def _block_ones(n, blk):
    idx = np.arange(n) // blk
    return jnp.asarray(idx[:, None] == idx[None, :], BF16)


def _layer(x2, bsz, seq, p, tm_proj, tm_mix):
    t = bsz * seq
    row = lambda a: a.reshape(1, -1).astype(F32)

    w_in = p['w_in'].astype(BF16)
    wg = w_in[:, :GLA_IN_PAD]
    wr = w_in[:, GLA_IN:]
    wgk = jnp.pad(p['w_gk_up'], ((0, 128 - GLA_GATE_RANK), (0, 0))).astype(BF16)
    wlora = jnp.zeros((LORA_IN, 3 * RWKV_WIDTH), F32)
    wlora = wlora.at[0:64, 0:RWKV_WIDTH].set(p['rwkv_w_up'])
    wlora = wlora.at[64:128, RWKV_WIDTH:2 * RWKV_WIDTH].set(p['rwkv_a_up'])
    wlora = wlora.at[128:256, 2 * RWKV_WIDTH:].set(p['rwkv_g_up']).astype(BF16)
    ones_r = _block_ones(RWKV_WIDTH, RWKV_HEAD)

    n_tiles = t // tm_proj
    tok = lambda w: pl.BlockSpec((tm_proj, w), lambda i: (i, 0))
    bshape = lambda w: jax.ShapeDtypeStruct((t, w), BF16)
    proj_out = pl.pallas_call(
        functools.partial(_proj_kernel, tiles_per_seq=seq // tm_proj),
        grid=(n_tiles,),
        in_specs=[tok(D_MODEL), _const_spec((D_MODEL, GLA_IN_PAD)), _const_spec((D_MODEL, RWKV_IN)),
                  _const_spec((1, RWKV_IN)), _const_spec((128, GLA_QK)), _const_spec((1, GLA_QK)),
                  _const_spec((LORA_IN, 3 * RWKV_WIDTH))] + [_const_spec((1, RWKV_WIDTH))] * 5
                 + [_const_spec((RWKV_WIDTH, RWKV_WIDTH))],
        out_specs=[tok(GLA_QK), tok(GLA_QK), tok(GLA_WIDTH), tok(GLA_WIDTH), tok(GLA_QK)]
                  + [tok(RWKV_WIDTH)] * 8,
        out_shape=[bshape(GLA_QK), bshape(GLA_QK), bshape(GLA_WIDTH), bshape(GLA_WIDTH),
                   jax.ShapeDtypeStruct((t, GLA_QK), F32)]
                  + [bshape(RWKV_WIDTH)] * 5 + [jax.ShapeDtypeStruct((t, RWKV_WIDTH), F32)]
                  + [bshape(RWKV_WIDTH)] * 2,
        scratch_shapes=[pltpu.VMEM((1, RWKV_IN), F32)],
        compiler_params=pltpu.CompilerParams(dimension_semantics=("arbitrary",),
                                             vmem_limit_bytes=VMEM_LIMIT_V7X),
        name="proj_prep",
    )(x2, wg, wr, row(p['mu_shift']), wgk, row(p['b_gk']), wlora, row(p['rwkv_w0']), row(p['rwkv_a0']),
      row(p['rwkv_k_k']), row(p['rwkv_k_a']), row(p['rwkv_r_k']), ones_r)
    gq, gk_, gv, gsg, ggk, rr, rk2, rv, rkk, rkb, rlw, rg, rbonus = proj_out

    def block_tri2(n_sub):
        tri = np.kron(np.eye(n_sub), np.tril(np.ones((CHUNK, CHUNK))))
        return jnp.asarray(np.concatenate([tri, tri], axis=1), BF16)

    rid = np.arange(SUB * GLA_QK)
    cid = SUB * ((rid % GLA_QK) // GLA_DK) + (SUB - 1) - rid // GLA_QK
    e_big = jnp.asarray(cid[:, None] == np.arange(128)[None, :], BF16)

    n_seq = max(n for n in (1, 2, 4, MIXER_SEQS) if bsz % n == 0)
    seq_group = min(MIXER_SEQ_GROUP, n_seq)
    sspec = lambda w: pl.BlockSpec((n_seq, CHUNK, w), lambda b, n: (b, n, 0))
    seq3 = lambda a: a.reshape(bsz, seq, a.shape[-1])
    o_a, o_b = pl.pallas_call(
        functools.partial(_mixers_kernel, n_seq=n_seq, seq_group=seq_group),
        grid=(bsz // n_seq, seq // CHUNK),
        in_specs=[sspec(GLA_QK), sspec(GLA_QK), sspec(GLA_WIDTH), sspec(GLA_WIDTH), sspec(GLA_QK),
                  _const_spec((1, GLA_WIDTH)), _const_spec((SUB * GLA_QK, 128))]
                 + [sspec(RWKV_WIDTH)] * 8 + [_const_spec((1, RWKV_WIDTH))] * 2
                 + [_const_spec((seq_group * CHUNK, 2 * seq_group * CHUNK)), _const_spec((RWKV_WIDTH, RWKV_WIDTH))],
        out_specs=[sspec(GLA_WIDTH), sspec(RWKV_WIDTH)],
        out_shape=[jax.ShapeDtypeStruct((bsz, seq, GLA_WIDTH), BF16),
                   jax.ShapeDtypeStruct((bsz, seq, RWKV_WIDTH), BF16)],
        scratch_shapes=[pltpu.VMEM((n_seq, GLA_DV, GLA_QK), F32),
                        pltpu.VMEM((SUB + n_seq * CHUNK, GLA_QK), F32),
                        pltpu.VMEM((SUB + n_seq * CHUNK, GLA_QK), F32),
                        pltpu.VMEM((n_seq, RWKV_HEAD, RWKV_WIDTH), F32),
                        pltpu.VMEM((n_seq * CHUNK, RWKV_WIDTH), F32)],
        compiler_params=pltpu.CompilerParams(dimension_semantics=("arbitrary", "arbitrary"),
                                             vmem_limit_bytes=VMEM_LIMIT_V7X),
        name="token_mixers",
    )(seq3(gq), seq3(gk_), seq3(gv), seq3(gsg), seq3(ggk), row(jnp.tile(p['gla_norm_w'], GLA_HEADS)), e_big,
      seq3(rr), seq3(rk2), seq3(rv), seq3(rkk), seq3(rkb), seq3(rlw), seq3(rg), seq3(rbonus),
      row(p['rwkv_gn_w']), row(p['rwkv_gn_b']), block_tri2(seq_group), ones_r)
    o_a = o_a.reshape(t, GLA_WIDTH)
    o_b = o_b.reshape(t, RWKV_WIDTH)

    mtok = lambda w: pl.BlockSpec((tm_mix, w), lambda i: (i, 0))
    out = pl.pallas_call(
        _mix_kernel,
        grid=(t // tm_mix,),
        in_specs=[mtok(D_MODEL), mtok(GLA_WIDTH), mtok(RWKV_WIDTH),
                  _const_spec((D_MODEL, 2 * D_MODEL)), _const_spec((1, 2 * D_MODEL)),
                  _const_spec((GLA_WIDTH, D_MODEL)), _const_spec((RWKV_WIDTH, D_MODEL)),
                  _const_spec((D_MODEL, D_MODEL)), _const_spec((1, D_MODEL)), _const_spec((1, D_MODEL)),
                  _const_spec((D_MODEL, D_FF)), _const_spec((1, D_FF)),
                  _const_spec((D_FF, D_MODEL)), _const_spec((1, D_MODEL)),
                  _const_spec((1, D_MODEL)), _const_spec((1, D_MODEL))],
        out_specs=mtok(D_MODEL),
        out_shape=jax.ShapeDtypeStruct((t, D_MODEL), F32),
        compiler_params=pltpu.CompilerParams(dimension_semantics=("arbitrary",),
                                             vmem_limit_bytes=VMEM_LIMIT_V7X),
        name="merge_mlp",
    )(x2, o_a, o_b, p['w_merge'].astype(BF16), row(p['b_merge']),
      p['w_branch'][0].astype(BF16), p['w_branch'][1].astype(BF16), p['w_out'].astype(BF16),
      row(p['ln1_g']), row(p['ln1_b']), p['w_mlp_up'].astype(BF16), row(p['b_mlp_up']),
      p['w_mlp_down'].astype(BF16), row(p['b_mlp_down']), row(p['ln2_g']), row(p['ln2_b']))
    return out


def kernel(x, w_in, mu_shift, w_gk_up, b_gk, gla_norm_w, rwkv_w0, rwkv_w_up, rwkv_a0, rwkv_a_up,
           rwkv_g_up, rwkv_k_k, rwkv_k_a, rwkv_r_k, rwkv_gn_w, rwkv_gn_b, w_merge, b_merge, w_branch,
           w_out, ln1_g, ln1_b, w_mlp_up, b_mlp_up, w_mlp_down, b_mlp_down, ln2_g, ln2_b):
    bsz, seq, d = x.shape
    assert d == D_MODEL and seq % DENSE_SUB == 0
    tm_proj = PROJ_TILE if seq % PROJ_TILE == 0 else DENSE_SUB
    tm_mix = MIX_TILE if (bsz * seq) % MIX_TILE == 0 else DENSE_SUB
    params = dict(w_in=w_in, mu_shift=mu_shift, w_gk_up=w_gk_up, b_gk=b_gk, gla_norm_w=gla_norm_w,
                  rwkv_w0=rwkv_w0, rwkv_w_up=rwkv_w_up, rwkv_a0=rwkv_a0, rwkv_a_up=rwkv_a_up,
                  rwkv_g_up=rwkv_g_up, rwkv_k_k=rwkv_k_k, rwkv_k_a=rwkv_k_a, rwkv_r_k=rwkv_r_k,
                  rwkv_gn_w=rwkv_gn_w, rwkv_gn_b=rwkv_gn_b, w_merge=w_merge, b_merge=b_merge,
                  w_branch=w_branch, w_out=w_out, ln1_g=ln1_g, ln1_b=ln1_b, w_mlp_up=w_mlp_up,
                  b_mlp_up=b_mlp_up, w_mlp_down=w_mlp_down, b_mlp_down=b_mlp_down, ln2_g=ln2_g, ln2_b=ln2_b)
    x2 = x.reshape(bsz * seq, d)
    for l in range(w_in.shape[0]):
        x2 = _layer(x2, bsz, seq, {n: a[l] for n, a in params.items()}, tm_proj, tm_mix)
    return x2.reshape(bsz, seq, d)
```

```python
import functools

import jax
import jax.numpy as jnp
import numpy as np
from jax import lax
from jax.experimental import pallas as pl
from jax.experimental.pallas import tpu as pltpu

F32 = jnp.float32
BF16 = jnp.bfloat16

D_MODEL = 1024
D_FF = 4 * D_MODEL
DEPTH = 1
CHUNK = 64
SUB = 16
GLA_HEADS, GLA_DK, GLA_DV = 4, 64, 128
GLA_GATE_RANK = 16
GLA_GATE_NORMALIZER = 16.0
GLA_NORM_EPS = 1e-5
RWKV_HEADS, RWKV_HEAD = 8, 64
RWKV_GN_EPS = 64e-5
L2_EPS = 1e-12
LN_EPS = 1e-5
ALPHA = (2.0 * DEPTH) ** 0.25
GLA_QK = GLA_HEADS * GLA_DK
GLA_WIDTH = GLA_HEADS * GLA_DV
RWKV_WIDTH = RWKV_HEADS * RWKV_HEAD
LANES = 128
GLA_GK_OFF = 2 * GLA_QK + 2 * GLA_WIDTH
GLA_IN = GLA_GK_OFF + GLA_GATE_RANK
GLA_IN_PAD = GLA_GK_OFF + LANES
RWKV_W_LORA, RWKV_A_LORA, RWKV_G_LORA = 64, 64, 128
LORA_OFF = 3 * RWKV_WIDTH
LORA_IN = RWKV_W_LORA + RWKV_A_LORA + RWKV_G_LORA
RWKV_IN = LORA_OFF + LORA_IN
RWKV_PAIR = 2 * RWKV_HEAD
FF_CHUNK = 1024
DENSE_SUB = 256
PROJ_TILE = 1024
MIX_TILE = 512
MIXER_SEQS = 8
MIXER_SEQ_GROUP = 4
LOG2_E = float(np.log2(np.e))
GLA_SAFE_LOG2 = 96.0
VMEM_LIMIT_V7X = 56 * 1024 * 1024


def _dot(a, b):
    return jnp.dot(a, b, preferred_element_type=F32)


def _dot_nt(a, b):
    return lax.dot_general(a, b, (((1,), (1,)), ((), ())), preferred_element_type=F32)


def _dot_tn(a, b):
    return lax.dot_general(a, b, (((0,), (0,)), ((), ())), preferred_element_type=F32)


def _sigmoid(z):
    return 1.0 / (1.0 + jnp.exp(-z))


def _log_sigmoid(z):
    return jnp.minimum(z, 0.0) - jnp.log(1.0 + jnp.exp(-jnp.abs(z)))


def _split_bf16(a):
    hi = a.astype(BF16)
    lo = (a - hi.astype(F32)).astype(BF16)
    return hi, lo


def _const_spec(shape):
    nd = len(shape)
    return pl.BlockSpec(shape, lambda *_: (0,) * nd, pipeline_mode=pl.Buffered(1))


def _proj_kernel(x_ref, wg_ref, wr_ref, mu_ref, wgk_ref, bgk_ref, wlora_ref, w0_ref, a0_ref,
                 kk_ref, ka_ref, rk_ref, ones_ref,
                 q_ref, k_ref, v_ref, sg_ref, gk_ref,
                 r_ref, rk2_ref, rv_ref, kkn_ref, kb_ref, lw_ref, g_ref, bonus_ref,
                 carry_ref, *, tiles_per_seq):
    i = pl.program_id(0)
    n_sub = x_ref.shape[0] // DENSE_SUB

    @pl.when(i % tiles_per_seq == 0)
    def _():
        carry_ref[...] = jnp.zeros_like(carry_ref)

    def project(j):
        xb = x_ref[j * DENSE_SUB:(j + 1) * DENSE_SUB, :].astype(BF16)
        return _dot(xb, wg_ref[...]), _dot(xb, wr_ref[...])

    def prepare(j, hg, hr, last_row):
        rows = slice(j * DENSE_SUB, (j + 1) * DENSE_SUB)
        q_ref[rows, :] = (hg[:, 0:GLA_QK] * (GLA_DK ** -0.5)).astype(BF16)
        k_ref[rows, :] = hg[:, GLA_QK:2 * GLA_QK].astype(BF16)
        v_ref[rows, :] = hg[:, 2 * GLA_QK:2 * GLA_QK + GLA_WIDTH].astype(BF16)
        g_out = hg[:, 2 * GLA_QK + GLA_WIDTH:2 * GLA_QK + 2 * GLA_WIDTH]
        sg_ref[rows, :] = (g_out * _sigmoid(g_out)).astype(BF16)
        z = _dot(hg[:, GLA_GK_OFF:GLA_IN_PAD].astype(BF16), wgk_ref[...]) + bgk_ref[...]
        gk_ref[rows, :] = _log_sigmoid(z) * (1.0 / GLA_GATE_NORMALIZER)

        row = lax.broadcasted_iota(jnp.int32, (DENSE_SUB, 1), 0)
        prev = jnp.where(row == 0, last_row, pltpu.roll(hr, 1, 0))
        u = hr + (prev - hr) * mu_ref[...]
        r = u[:, 0:RWKV_WIDTH]
        k = u[:, RWKV_WIDTH:2 * RWKV_WIDTH]
        v = u[:, 2 * RWKV_WIDTH:3 * RWKV_WIDTH]
        low = u[:, LORA_OFF:LORA_OFF + LORA_IN]
        col = lax.broadcasted_iota(jnp.int32, (1, LORA_IN), 1)
        low = jnp.where(col < RWKV_W_LORA, jnp.tanh(low),
                        jnp.where(col < RWKV_W_LORA + RWKV_A_LORA, low, _sigmoid(low)))
        lo = _dot(low.astype(BF16), wlora_ref[...])
        lw_ref[rows, :] = -float(np.exp(-0.5)) * _sigmoid(w0_ref[...] + lo[:, 0:RWKV_WIDTH])
        a = _sigmoid(a0_ref[...] + lo[:, RWKV_WIDTH:2 * RWKV_WIDTH])
        g_ref[rows, :] = lo[:, 2 * RWKV_WIDTH:3 * RWKV_WIDTH].astype(BF16)

        ones = ones_ref[...]
        kk = k * kk_ref[...]
        ss = _dot((kk * kk).astype(BF16), ones)
        kkn = kk / jnp.maximum(jnp.sqrt(ss), L2_EPS)
        k2 = k * (1.0 + (a - 1.0) * ka_ref[...])
        bonus = _dot((r * k2 * rk_ref[...]).astype(BF16), ones) * v
        r_ref[rows, :] = r.astype(BF16)
        rk2_ref[rows, :] = k2.astype(BF16)
        rv_ref[rows, :] = v.astype(BF16)
        kkn_ref[rows, :] = kkn.astype(BF16)
        kb_ref[rows, :] = (kkn * a).astype(BF16)
        bonus_ref[rows, :] = bonus.astype(BF16)
        return hr[DENSE_SUB - 1:DENSE_SUB, :]

    last_row = carry_ref[...]
    pending = project(0)
    for j in range(n_sub):
        following = project(j + 1) if j + 1 < n_sub else None
        last_row = prepare(j, *pending, last_row)
        pending = following
    carry_ref[...] = last_row


def _gla_part(q_ref, k_ref, v_ref, sg_ref, gk_ref, nw_ref, tri_ref, ebig_ref, o_ref,
              s_ref, kp_ref, bp_ref, *, n_seq, seq_group):
    blk = n_seq * CHUNK
    heads = range(GLA_HEADS)
    subs = range(n_seq)
    hsl = [slice(h * GLA_DK, (h + 1) * GLA_DK) for h in heads]
    vsl = [slice(h * GLA_DV, (h + 1) * GLA_DV) for h in heads]
    rows = [slice(g * CHUNK, (g + 1) * CHUNK) for g in subs]
    n_pair = GLA_HEADS // 2
    ksl = [slice(p * 2 * GLA_DK, (p + 1) * 2 * GLA_DK) for p in range(n_pair)]
    wsl = [slice(p * 2 * GLA_DV, (p + 1) * 2 * GLA_DV) for p in range(n_pair)]
    chains = [(g, p) for g in subs for p in range(n_pair)]
    lane_k = lax.broadcasted_iota(jnp.int32, (1, GLA_QK), 1)
    lane_v = lax.broadcasted_iota(jnp.int32, (1, GLA_WIDTH), 1)
    first_k = (lane_k & GLA_DK) == 0
    first_v = (lane_v & GLA_DV) == 0
    first_s = lax.broadcasted_iota(jnp.int32, (GLA_DV, 2 * GLA_DK), 1) < GLA_DK
    lane_c = lax.broadcasted_iota(jnp.int32, (CHUNK, 2 * GLA_DK), 1)
    causal = lax.broadcasted_iota(jnp.int32, (CHUNK, 2 * GLA_DK), 0) >= (lane_c & (GLA_DK - 1))
    row_c = lax.broadcasted_iota(jnp.int32, (CHUNK, 1), 0)

    def load(ref):
        return ref[...].reshape(blk, ref.shape[-1])

    gk = load(gk_ref)
    n_rows = seq_group * CHUNK
    b = jnp.concatenate([_dot(tri_ref[...], jnp.concatenate(_split_bf16(gk[j * n_rows:(j + 1) * n_rows, :]), axis=0))
                         for j in range(n_seq // seq_group)], axis=0) * LOG2_E

    def finish(k, v_all, qe, score):
        b_last = {g: b[(g + 1) * CHUNK - 1:(g + 1) * CHUNK, :] for g in subs}
        kdec = {g: (k[rows[g], :] * jnp.exp2(b_last[g] - b[rows[g], :])).astype(BF16) for g in subs}
        kv = {(g, p): _dot_tn(v_all[rows[g], wsl[p]], kdec[g][:, ksl[p]]) for g, p in chains}
        state = {(g, p): s_ref[g, :, ksl[p]] for g, p in chains}
        entering = {ch: jnp.concatenate([jnp.where(first_s, state[ch], 0.0),
                                         jnp.where(first_s, 0.0, state[ch])], axis=0).astype(BF16) for ch in chains}
        new_state = {(g, p): state[g, p] * jnp.exp2(b_last[g])[:, ksl[p]]
                     + jnp.where(first_s, kv[g, p][0:GLA_DV], kv[g, p][GLA_DV:]) for g, p in chains}
        inter = {(g, p): _dot_nt(qe[rows[g], ksl[p]], entering[g, p]) for g, p in chains}
        sg = load(sg_ref)
        outs = []
        for g in subs:
            pieces = []
            for p in range(n_pair):
                op = inter[g, p] + score[g, p]
                for oh in (op[:, 0:GLA_DV], op[:, GLA_DV:]):
                    ms = jnp.mean(oh * oh, axis=-1, keepdims=True)
                    pieces.append(oh * lax.rsqrt(ms + GLA_NORM_EPS))
            o = jnp.concatenate(pieces, axis=-1) * nw_ref[...]
            outs.append((o * sg[rows[g], :].astype(F32)).astype(BF16))
        o_ref[...] = jnp.stack(outs, axis=0)
        return new_state

    safe = jnp.min(b) >= -GLA_SAFE_LOG2

    q = load(q_ref).astype(F32)
    k = load(k_ref).astype(F32)
    v_all = load(v_ref)
    qe = (q * jnp.exp2(b)).astype(BF16)
    kn = (k * jnp.exp2(-b)).astype(BF16)
    kn_1, kn_2 = kn * jnp.where(first_k, 1.0, 0.0).astype(BF16), kn * jnp.where(first_k, 0.0, 1.0).astype(BF16)
    v_1, v_2 = v_all * jnp.where(first_v, 1.0, 0.0).astype(BF16), v_all * jnp.where(first_v, 0.0, 1.0).astype(BF16)
    raw = {(g, p): _dot_nt(qe[rows[g], ksl[p]],
                           jnp.concatenate([kn_1[rows[g], ksl[p]], kn_2[rows[g], ksl[p]]], axis=0))
           for g, p in chains}
    sc_fast = {ch: jnp.where(causal, raw[ch], 0.0).astype(BF16) for ch in chains}
    state_fast = finish(k, v_all, qe, {
        (g, p): _dot(sc_fast[g, p], jnp.concatenate([v_1[rows[g], wsl[p]], v_2[rows[g], wsl[p]]], axis=0))
        for g, p in chains})

    @pl.when(safe)
    def _():
        for g, p in chains:
            s_ref[g, :, ksl[p]] = state_fast[g, p]

    @pl.when(jnp.logical_not(safe))
    def _():
        kp_ref[0:SUB, :] = jnp.zeros((SUB, GLA_QK), F32)
        bp_ref[0:SUB, :] = jnp.zeros((SUB, GLA_QK), F32)
        kp_ref[SUB:SUB + blk, :] = k
        bp_ref[SUB:SUB + blk, :] = b
        rmod = lax.broadcasted_iota(jnp.int32, (blk, 1), 0) % SUB
        xs = [(q * k).astype(BF16)]
        for d in range(1, SUB):
            ks = kp_ref[SUB - d:SUB - d + blk, :]
            bs = bp_ref[SUB - d:SUB - d + blk, :]
            xs.append(jnp.where(rmod >= d, q * ks * jnp.exp2(b - bs), 0.0).astype(BF16))
        s_diag = _dot(jnp.concatenate(xs, axis=1), ebig_ref[...])
        lane = lax.broadcasted_iota(jnp.int32, (1, LANES), 1)
        score = {}
        for g in subs:
            bg = b[rows[g], :]
            for h in heads:
                qh, kh, bh = q[rows[g], hsl[h]], k[rows[g], hsl[h]], bg[:, hsl[h]]
                md = jnp.where((lane >= SUB * h) & (lane < SUB * (h + 1)), s_diag[rows[g], :], 0.0)
                sc_rows = [jnp.zeros((SUB, CHUNK), F32)]
                for sb in range(1, CHUNK // SUB):
                    lo_r = sb * SUB
                    ref = bh[lo_r - 1:lo_r, :]
                    qi = qh[lo_r:lo_r + SUB, :] * jnp.exp2(bh[lo_r:lo_r + SUB, :] - ref)
                    ki = jnp.where(row_c < lo_r, kh * jnp.exp2(jnp.minimum(ref - bh, 0.0)), 0.0)
                    sc_rows.append(_dot_nt(qi.astype(BF16), ki.astype(BF16)))
                sc = jnp.concatenate(sc_rows, axis=0) + pltpu.roll(
                    md, LANES - (SUB - 1) - SUB * h, 1, stride=1, stride_axis=0)[:, 0:CHUNK]
                score[g, h] = _dot(sc.astype(BF16), v_all[rows[g], vsl[h]])
        state_slow = finish(k, v_all, qe, {(g, p): jnp.concatenate([score[g, 2 * p], score[g, 2 * p + 1]], axis=1)
                                          for g, p in chains})
        for g, p in chains:
            s_ref[g, :, ksl[p]] = state_slow[g, p]


def _rwkv_part(r_ref, k_ref, v_ref, kk_ref, kb_ref, lw_ref, g_ref, bonus_ref, gnw_ref, gnb_ref,
               tri_ref, ones_ref, o_ref, s_ref, y_ref, *, n_seq, seq_group):
    n_pair = RWKV_HEADS // 2
    psl = [slice(p * RWKV_PAIR, (p + 1) * RWKV_PAIR) for p in range(n_pair)]
    subs = range(seq_group)
    rows = [slice(g * CHUNK, (g + 1) * CHUNK) for g in subs]
    chains = [(g, p) for g in subs for p in range(n_pair)]
    n_rows = seq_group * CHUNK

    lane_w = lax.broadcasted_iota(jnp.int32, (1, RWKV_WIDTH), 1)
    first_w = (lane_w & RWKV_HEAD) == 0

    keep_1 = jnp.where(first_w, 1.0, 0.0).astype(BF16)
    keep_2 = jnp.where(first_w, 0.0, 1.0).astype(BF16)

    def halves(a):
        return a * keep_1, a * keep_2

    lane_p = lax.broadcasted_iota(jnp.int32, (CHUNK, RWKV_PAIR), 1)
    row_p = lax.broadcasted_iota(jnp.int32, (CHUNK, RWKV_PAIR), 0)
    first_p = lane_p < RWKV_HEAD
    col_p = lane_p & (RWKV_HEAD - 1)
    strict = row_p > col_p
    incl = row_p >= col_p
    eye = (row_p == col_p).astype(F32)

    def bdiag(a):
        return jnp.concatenate([jnp.where(first_p, a, 0.0), jnp.where(first_p, 0.0, a)], axis=0).astype(BF16)

    def cut(a, ch):
        return a[rows[ch[0]], psl[ch[1]]]

    def prepare(grp):
        def load(ref):
            return ref[grp * seq_group:(grp + 1) * seq_group].reshape(n_rows, RWKV_WIDTH)

        lw = load(lw_ref)
        c = _dot(tri_ref[...], jnp.concatenate(_split_bf16(lw), axis=0))
        e_nc = jnp.exp(-c)
        bt_f = load(kb_ref).astype(F32) * e_nc
        kt_f = load(k_ref).astype(F32) * e_nc
        v_all = load(v_ref)
        at_all = (-load(kk_ref).astype(F32) * jnp.exp(c - lw)).astype(BF16)
        d = dict(v_all=v_all, at_all=at_all, rt_all=(load(r_ref).astype(F32) * jnp.exp(c)).astype(BF16))
        d['at_1'], d['at_2'] = halves(at_all)
        d['bt_1'], d['bt_2'] = halves(bt_f.astype(BF16))
        d['kt_1'], d['kt_2'] = halves(kt_f.astype(BF16))
        d['v_1'], d['v_2'] = halves(v_all)
        d['e_last'] = {g: jnp.exp(c[(g + 1) * CHUNK - 1:(g + 1) * CHUNK, :]) for g in subs}
        d['bh'] = {g: (bt_f[rows[g], :] * d['e_last'][g]).astype(BF16) for g in subs}
        d['kh'] = {g: (kt_f[rows[g], :] * d['e_last'][g]).astype(BF16) for g in subs}
        return d

    def first_level(d):
        return {ch: _dot_nt(jnp.concatenate([cut(d['at_all'], ch), cut(d['rt_all'], ch)], axis=0),
                            jnp.concatenate([cut(d['bt_1'], ch), cut(d['bt_2'], ch),
                                             cut(d['kt_1'], ch), cut(d['kt_2'], ch)], axis=0))
                for ch in chains}

    def run_chains(grp, d, a_mat):
        a_ab = {ch: jnp.where(strict, a_mat[ch][0:CHUNK, 0:RWKV_PAIR], 0.0) for ch in chains}
        a_kr = {ch: jnp.concatenate([jnp.where(strict, a_mat[ch][0:CHUNK, RWKV_PAIR:], 0.0),
                                     jnp.where(incl, a_mat[ch][CHUNK:, RWKV_PAIR:], 0.0)], axis=0).astype(BF16)
                for ch in chains}
        a_rb = {ch: jnp.where(incl, a_mat[ch][CHUNK:, 0:RWKV_PAIR], 0.0).astype(BF16) for ch in chains}
        av = {ch: _dot(a_kr[ch], jnp.concatenate([cut(d['v_1'], ch), cut(d['v_2'], ch)], axis=0)) for ch in chains}
        t_inv = {ch: eye + a_ab[ch] for ch in chains}
        pw = {ch: _dot(a_ab[ch].astype(BF16), bdiag(a_ab[ch])) for ch in chains}
        for _ in range(4):
            pt = {ch: _dot(jnp.concatenate([pw[ch], t_inv[ch]], axis=0).astype(BF16), bdiag(pw[ch]))
                  for ch in chains}
            pw = {ch: pt[ch][0:CHUNK] for ch in chains}
            t_inv = {ch: t_inv[ch] + pt[ch][CHUNK:] for ch in chains}
        t_inv = {ch: t_inv[ch] + _dot(t_inv[ch].astype(BF16), bdiag(pw[ch])) for ch in chains}
        wu = {ch: _dot(t_inv[ch].astype(BF16),
                       jnp.concatenate([jnp.concatenate([cut(d['at_1'], ch), cut(d['at_2'], ch)], axis=0),
                                        bdiag(av[ch][0:CHUNK])], axis=1)) for ch in chains}
        wr_til = {ch: jnp.concatenate([wu[ch][:, 0:RWKV_PAIR].astype(BF16), cut(d['rt_all'], ch)], axis=0)
                  for ch in chains}
        state = {(g, p): s_ref[grp * seq_group + g, :, psl[p]] for g, p in chains}
        ws = {ch: _dot_nt(wr_til[ch], bdiag(state[ch])) for ch in chains}
        u = {ch: ws[ch][0:CHUNK] + wu[ch][:, RWKV_PAIR:] for ch in chains}
        for g, p in chains:
            y_ref[grp * n_rows + g * CHUNK:grp * n_rows + (g + 1) * CHUNK, psl[p]] = (
                ws[g, p][CHUNK:] + _dot(a_rb[g, p], bdiag(u[g, p])) + av[g, p][CHUNK:])
        upd = {(g, p): _dot_tn(jnp.concatenate([u[g, p].astype(BF16), d['v_all'][rows[g], psl[p]]], axis=0),
                               jnp.concatenate([d['bh'][g][:, psl[p]], d['kh'][g][:, psl[p]]], axis=0))
               for g, p in chains}
        for g, p in chains:
            s_ref[grp * seq_group + g, :, psl[p]] = (state[g, p] * d['e_last'][g][:, psl[p]]
                                                     + jnp.where(first_p, upd[g, p][0:CHUNK], upd[g, p][CHUNK:]))

    def normalise(grp):
        ones = ones_ref[...]
        y = y_ref[grp * n_rows:(grp + 1) * n_rows, :]
        mu = _dot(y.astype(BF16), ones) * (1.0 / RWKV_HEAD)
        yc = y - mu
        var = _dot((yc * yc).astype(BF16), ones) * (1.0 / RWKV_HEAD)
        yn = yc * lax.rsqrt(var + RWKV_GN_EPS) * gnw_ref[...] + gnb_ref[...]
        gs = slice(grp * seq_group, (grp + 1) * seq_group)
        o = ((yn + bonus_ref[gs].reshape(n_rows, RWKV_WIDTH).astype(F32))
             * g_ref[gs].reshape(n_rows, RWKV_WIDTH).astype(F32)).astype(BF16)
        o_ref[gs] = o.reshape(seq_group, CHUNK, RWKV_WIDTH)

    n_grp = n_seq // seq_group
    pending = prepare(0)
    for grp in range(n_grp):
        following = prepare(grp + 1) if grp + 1 < n_grp else None
        run_chains(grp, pending, first_level(pending))
        if grp > 0:
            normalise(grp - 1)
        pending = following
    normalise(n_grp - 1)


def _mixers_kernel(gq_ref, gk_ref, gv_ref, gsg_ref, ggk_ref, gnw_ref, ebig_ref,
                   r_ref, k_ref, v_ref, kk_ref, kb_ref, lw_ref, g_ref, bonus_ref, rnw_ref, rnb_ref,
                   tri_ref, ones_ref, oa_ref, ob_ref,
                   gs_ref, kp_ref, bp_ref, rs_ref, y_ref, *, n_seq, seq_group):
    @pl.when(pl.program_id(1) == 0)
    def _():
        gs_ref[...] = jnp.zeros_like(gs_ref)
        rs_ref[...] = jnp.zeros_like(rs_ref)

    _rwkv_part(r_ref, k_ref, v_ref, kk_ref, kb_ref, lw_ref, g_ref, bonus_ref, rnw_ref, rnb_ref,
               tri_ref, ones_ref, ob_ref, rs_ref, y_ref, n_seq=n_seq, seq_group=seq_group)
    _gla_part(gq_ref, gk_ref, gv_ref, gsg_ref, ggk_ref, gnw_ref, tri_ref, ebig_ref, oa_ref,
              gs_ref, kp_ref, bp_ref, n_seq=n_seq, seq_group=seq_group)


def _layernorm(h, g, b):
    mu = jnp.mean(h, axis=-1, keepdims=True)
    hc = h - mu
    var = jnp.mean(hc * hc, axis=-1, keepdims=True)
    return hc * lax.rsqrt(var + LN_EPS) * g + b


def _mix_kernel(x_ref, oa_ref, ob_ref, wm_ref, bm_ref, wba_ref, wbb_ref, wout_ref, ln1g_ref, ln1b_ref,
                w1_ref, b1_ref, w2_ref, b2_ref, ln2g_ref, ln2b_ref, out_ref):
    n_sub = x_ref.shape[0] // DENSE_SUB

    def merge(j):
        rows = slice(j * DENSE_SUB, (j + 1) * DENSE_SUB)
        x = x_ref[rows, :]
        xb = x.astype(BF16)
        ya = _dot(oa_ref[rows, :], wba_ref[...])
        yb = _dot(ob_ref[rows, :], wbb_ref[...])
        ga = _sigmoid(_dot(xb, wm_ref[:, 0:D_MODEL]) + bm_ref[:, 0:D_MODEL])
        m = ga * ya
        gb = _sigmoid(_dot(xb, wm_ref[:, D_MODEL:]) + bm_ref[:, D_MODEL:])
        m = m + gb * yb
        return ALPHA * x + _dot(m.astype(BF16), wout_ref[...])

    def mlp(j, z):
        rows = slice(j * DENSE_SUB, (j + 1) * DENSE_SUB)
        x1 = _layernorm(z, ln1g_ref[...], ln1b_ref[...])
        x1b = x1.astype(BF16)
        acc = jnp.zeros_like(x1)
        for c in range(D_FF // FF_CHUNK):
            cs = slice(c * FF_CHUNK, (c + 1) * FF_CHUNK)
            h = jnp.maximum(_dot(x1b, w1_ref[:, cs]) + b1_ref[:, cs], 0.0)
            acc = acc + _dot((h * h).astype(BF16), w2_ref[cs, :])
        out_ref[rows, :] = _layernorm(ALPHA * x1 + acc + b2_ref[...], ln2g_ref[...], ln2b_ref[...])

    pending = merge(0)
    for j in range(n_sub):
        following = merge(j + 1) if j + 1 < n_sub else None
        mlp(j, pending)
        pending = following


def _block_ones(n, blk):
    idx = np.arange(n) // blk
    return jnp.asarray(idx[:, None] == idx[None, :], BF16)


def _layer(x2, bsz, seq, p, tm_proj, tm_mix):
    t = bsz * seq
    row = lambda a: a.reshape(1, -1).astype(F32)

    w_in = p['w_in']
    wg = jnp.pad(w_in[:, :GLA_IN], ((0, 0), (0, GLA_IN_PAD - GLA_IN))).astype(BF16)
    wr = w_in[:, GLA_IN:].astype(BF16)
    wgk = jnp.pad(p['w_gk_up'], ((0, LANES - GLA_GATE_RANK), (0, 0))).astype(BF16)
    a_off = RWKV_W_LORA + RWKV_A_LORA
    wlora = jnp.zeros((LORA_IN, 3 * RWKV_WIDTH), F32)
    wlora = wlora.at[0:RWKV_W_LORA, 0:RWKV_WIDTH].set(p['rwkv_w_up'])
    wlora = wlora.at[RWKV_W_LORA:a_off, RWKV_WIDTH:2 * RWKV_WIDTH].set(p['rwkv_a_up'])
    wlora = wlora.at[a_off:LORA_IN, 2 * RWKV_WIDTH:].set(p['rwkv_g_up']).astype(BF16)
    ones_r = _block_ones(RWKV_WIDTH, RWKV_HEAD)

    n_tiles = t // tm_proj
    tok = lambda w: pl.BlockSpec((tm_proj, w), lambda i: (i, 0))
    bshape = lambda w: jax.ShapeDtypeStruct((t, w), BF16)
    proj_out = pl.pallas_call(
        functools.partial(_proj_kernel, tiles_per_seq=seq // tm_proj),
        grid=(n_tiles,),
        in_specs=[tok(D_MODEL), _const_spec((D_MODEL, GLA_IN_PAD)), _const_spec((D_MODEL, RWKV_IN)),
                  _const_spec((1, RWKV_IN)), _const_spec((LANES, GLA_QK)), _const_spec((1, GLA_QK)),
                  _const_spec((LORA_IN, 3 * RWKV_WIDTH))] + [_const_spec((1, RWKV_WIDTH))] * 5
                 + [_const_spec((RWKV_WIDTH, RWKV_WIDTH))],
        out_specs=[tok(GLA_QK), tok(GLA_QK), tok(GLA_WIDTH), tok(GLA_WIDTH), tok(GLA_QK)]
                  + [tok(RWKV_WIDTH)] * 8,
        out_shape=[bshape(GLA_QK), bshape(GLA_QK), bshape(GLA_WIDTH), bshape(GLA_WIDTH),
                   jax.ShapeDtypeStruct((t, GLA_QK), F32)]
                  + [bshape(RWKV_WIDTH)] * 5 + [jax.ShapeDtypeStruct((t, RWKV_WIDTH), F32)]
                  + [bshape(RWKV_WIDTH)] * 2,
        scratch_shapes=[pltpu.VMEM((1, RWKV_IN), F32)],
        compiler_params=pltpu.CompilerParams(dimension_semantics=("arbitrary",),
                                             vmem_limit_bytes=VMEM_LIMIT_V7X),
        name="proj_prep",
    )(x2, wg, wr, row(p['mu_shift']), wgk, row(p['b_gk']), wlora, row(p['rwkv_w0']), row(p['rwkv_a0']),
      row(p['rwkv_k_k']), row(p['rwkv_k_a']), row(p['rwkv_r_k']), ones_r)
    gq, gk_, gv, gsg, ggk, rr, rk2, rv, rkk, rkb, rlw, rg, rbonus = proj_out

    def block_tri2(n_sub):
        tri = np.kron(np.eye(n_sub), np.tril(np.ones((CHUNK, CHUNK))))
        return jnp.asarray(np.concatenate([tri, tri], axis=1), BF16)

    rid = np.arange(SUB * GLA_QK)
    cid = SUB * ((rid % GLA_QK) // GLA_DK) + (SUB - 1) - rid // GLA_QK
    e_big = jnp.asarray(cid[:, None] == np.arange(LANES)[None, :], BF16)

    n_seq = max(n for n in (1, 2, 4, MIXER_SEQS) if bsz % n == 0)
    seq_group = min(MIXER_SEQ_GROUP, n_seq)
    sspec = lambda w: pl.BlockSpec((n_seq, CHUNK, w), lambda b, n: (b, n, 0))
    seq3 = lambda a: a.reshape(bsz, seq, a.shape[-1])
    o_a, o_b = pl.pallas_call(
        functools.partial(_mixers_kernel, n_seq=n_seq, seq_group=seq_group),
        grid=(bsz // n_seq, seq // CHUNK),
        in_specs=[sspec(GLA_QK), sspec(GLA_QK), sspec(GLA_WIDTH), sspec(GLA_WIDTH), sspec(GLA_QK),
                  _const_spec((1, GLA_WIDTH)), _const_spec((SUB * GLA_QK, LANES))]
                 + [sspec(RWKV_WIDTH)] * 8 + [_const_spec((1, RWKV_WIDTH))] * 2
                 + [_const_spec((seq_group * CHUNK, 2 * seq_group * CHUNK)), _const_spec((RWKV_WIDTH, RWKV_WIDTH))],
        out_specs=[sspec(GLA_WIDTH), sspec(RWKV_WIDTH)],
        out_shape=[jax.ShapeDtypeStruct((bsz, seq, GLA_WIDTH), BF16),
                   jax.ShapeDtypeStruct((bsz, seq, RWKV_WIDTH), BF16)],
        scratch_shapes=[pltpu.VMEM((n_seq, GLA_DV, GLA_QK), F32),
                        pltpu.VMEM((SUB + n_seq * CHUNK, GLA_QK), F32),
                        pltpu.VMEM((SUB + n_seq * CHUNK, GLA_QK), F32),
                        pltpu.VMEM((n_seq, RWKV_HEAD, RWKV_WIDTH), F32),
                        pltpu.VMEM((n_seq * CHUNK, RWKV_WIDTH), F32)],
        compiler_params=pltpu.CompilerParams(dimension_semantics=("arbitrary", "arbitrary"),
                                             vmem_limit_bytes=VMEM_LIMIT_V7X),
        name="token_mixers",
    )(seq3(gq), seq3(gk_), seq3(gv), seq3(gsg), seq3(ggk), row(jnp.tile(p['gla_norm_w'], GLA_HEADS)), e_big,
      seq3(rr), seq3(rk2), seq3(rv), seq3(rkk), seq3(rkb), seq3(rlw), seq3(rg), seq3(rbonus),
      row(p['rwkv_gn_w']), row(p['rwkv_gn_b']), block_tri2(seq_group), ones_r)
    o_a = o_a.reshape(t, GLA_WIDTH)
    o_b = o_b.reshape(t, RWKV_WIDTH)

    mtok = lambda w: pl.BlockSpec((tm_mix, w), lambda i: (i, 0))
    out = pl.pallas_call(
        _mix_kernel,
        grid=(t // tm_mix,),
        in_specs=[mtok(D_MODEL), mtok(GLA_WIDTH), mtok(RWKV_WIDTH),
                  _const_spec((D_MODEL, 2 * D_MODEL)), _const_spec((1, 2 * D_MODEL)),
                  _const_spec((GLA_WIDTH, D_MODEL)), _const_spec((RWKV_WIDTH, D_MODEL)),
                  _const_spec((D_MODEL, D_MODEL)), _const_spec((1, D_MODEL)), _const_spec((1, D_MODEL)),
                  _const_spec((D_MODEL, D_FF)), _const_spec((1, D_FF)),
                  _const_spec((D_FF, D_MODEL)), _const_spec((1, D_MODEL)),
                  _const_spec((1, D_MODEL)), _const_spec((1, D_MODEL))],
        out_specs=mtok(D_MODEL),
        out_shape=jax.ShapeDtypeStruct((t, D_MODEL), F32),
        compiler_params=pltpu.CompilerParams(dimension_semantics=("arbitrary",),
                                             vmem_limit_bytes=VMEM_LIMIT_V7X),
        name="merge_mlp",
    )(x2, o_a, o_b, p['w_merge'].astype(BF16), row(p['b_merge']),
      p['w_branch'][0].astype(BF16), p['w_branch'][1].astype(BF16), p['w_out'].astype(BF16),
      row(p['ln1_g']), row(p['ln1_b']), p['w_mlp_up'].astype(BF16), row(p['b_mlp_up']),
      p['w_mlp_down'].astype(BF16), row(p['b_mlp_down']), row(p['ln2_g']), row(p['ln2_b']))
    return out


def kernel(x, w_in, mu_shift, w_gk_up, b_gk, gla_norm_w, rwkv_w0, rwkv_w_up, rwkv_a0, rwkv_a_up,
           rwkv_g_up, rwkv_k_k, rwkv_k_a, rwkv_r_k, rwkv_gn_w, rwkv_gn_b, w_merge, b_merge, w_branch,
           w_out, ln1_g, ln1_b, w_mlp_up, b_mlp_up, w_mlp_down, b_mlp_down, ln2_g, ln2_b):
    bsz, seq, d = x.shape
    assert d == D_MODEL and seq % DENSE_SUB == 0
    tm_proj = PROJ_TILE if seq % PROJ_TILE == 0 else DENSE_SUB
    tm_mix = MIX_TILE if (bsz * seq) % MIX_TILE == 0 else DENSE_SUB
    params = dict(w_in=w_in, mu_shift=mu_shift, w_gk_up=w_gk_up, b_gk=b_gk, gla_norm_w=gla_norm_w,
                  rwkv_w0=rwkv_w0, rwkv_w_up=rwkv_w_up, rwkv_a0=rwkv_a0, rwkv_a_up=rwkv_a_up,
                  rwkv_g_up=rwkv_g_up, rwkv_k_k=rwkv_k_k, rwkv_k_a=rwkv_k_a, rwkv_r_k=rwkv_r_k,
                  rwkv_gn_w=rwkv_gn_w, rwkv_gn_b=rwkv_gn_b, w_merge=w_merge, b_merge=b_merge,
                  w_branch=w_branch, w_out=w_out, ln1_g=ln1_g, ln1_b=ln1_b, w_mlp_up=w_mlp_up,
                  b_mlp_up=b_mlp_up, w_mlp_down=w_mlp_down, b_mlp_down=b_mlp_down, ln2_g=ln2_g, ln2_b=ln2_b)
    x2 = x.reshape(bsz * seq, d)
    for l in range(w_in.shape[0]):
        x2 = _layer(x2, bsz, seq, {n: a[l] for n, a in params.items()}, tm_proj, tm_mix)
    return x2.reshape(bsz, seq, d)
```

```python
import functools

import jax
import jax.numpy as jnp
import numpy as np
from jax import lax
from jax.experimental import pallas as pl
from jax.experimental.pallas import tpu as pltpu

F32 = jnp.float32
BF16 = jnp.bfloat16

D_MODEL = 1024
D_FF = 4 * D_MODEL
DEPTH = 1
CHUNK = 64
SUB = 16
GLA_HEADS, GLA_DK, GLA_DV = 4, 64, 128
GLA_GATE_RANK = 16
GLA_GATE_NORMALIZER = 16.0
GLA_NORM_EPS = 1e-5
RWKV_HEADS, RWKV_HEAD = 8, 64
RWKV_GN_EPS = 64e-5
L2_EPS = 1e-12
LN_EPS = 1e-5
ALPHA = (2.0 * DEPTH) ** 0.25
GLA_QK = GLA_HEADS * GLA_DK
GLA_WIDTH = GLA_HEADS * GLA_DV
RWKV_WIDTH = RWKV_HEADS * RWKV_HEAD
LANES = 128
GLA_GK_OFF = 2 * GLA_QK + 2 * GLA_WIDTH
GLA_IN = GLA_GK_OFF + GLA_GATE_RANK
GLA_IN_PAD = GLA_GK_OFF + LANES
RWKV_W_LORA, RWKV_A_LORA, RWKV_G_LORA = 64, 64, 128
LORA_OFF = 3 * RWKV_WIDTH
LORA_IN = RWKV_W_LORA + RWKV_A_LORA + RWKV_G_LORA
RWKV_IN = LORA_OFF + LORA_IN
RWKV_PAIR = 2 * RWKV_HEAD
FF_CHUNK = 1024
DENSE_SUB = 256
PROJ_TILE = 1024
MIX_TILE = 512
MIXER_SEQS = 8
MIXER_SEQ_GROUP = 4
MIXER_CHUNKS = 2
LOG2_E = float(np.log2(np.e))
GLA_SAFE_LOG2 = 96.0
VMEM_LIMIT_V7X = 56 * 1024 * 1024


def _dot(a, b):
    return jnp.dot(a, b, preferred_element_type=F32)


def _dot_nt(a, b):
    return lax.dot_general(a, b, (((1,), (1,)), ((), ())), preferred_element_type=F32)


def _dot_tn(a, b):
    return lax.dot_general(a, b, (((0,), (0,)), ((), ())), preferred_element_type=F32)


def _sigmoid(z):
    return 1.0 / (1.0 + jnp.exp(-z))


def _log_sigmoid(z):
    return jnp.minimum(z, 0.0) - jnp.log(1.0 + jnp.exp(-jnp.abs(z)))


def _split_bf16(a):
    hi = a.astype(BF16)
    lo = (a - hi.astype(F32)).astype(BF16)
    return hi, lo


def _const_spec(shape):
    nd = len(shape)
    return pl.BlockSpec(shape, lambda *_: (0,) * nd, pipeline_mode=pl.Buffered(1))


def _proj_kernel(x_ref, wg_ref, wr_ref, mu_ref, wgk_ref, bgk_ref, wlora_ref, w0_ref, a0_ref,
                 kk_ref, ka_ref, rk_ref, ones_ref,
                 q_ref, k_ref, v_ref, sg_ref, gk_ref,
                 r_ref, rk2_ref, rv_ref, kkn_ref, kb_ref, lw_ref, g_ref, bonus_ref,
                 carry_ref, *, tiles_per_seq):
    i = pl.program_id(0)
    n_sub = x_ref.shape[0] // DENSE_SUB

    @pl.when(i % tiles_per_seq == 0)
    def _():
        carry_ref[...] = jnp.zeros_like(carry_ref)

    def project(j):
        xb = x_ref[j * DENSE_SUB:(j + 1) * DENSE_SUB, :].astype(BF16)
        return _dot(xb, wg_ref[...]), _dot(xb, wr_ref[...])

    def prepare(j, hg, hr, last_row):
        rows = slice(j * DENSE_SUB, (j + 1) * DENSE_SUB)
        q_ref[rows, :] = (hg[:, 0:GLA_QK] * (GLA_DK ** -0.5)).astype(BF16)
        k_ref[rows, :] = hg[:, GLA_QK:2 * GLA_QK].astype(BF16)
        v_ref[rows, :] = hg[:, 2 * GLA_QK:2 * GLA_QK + GLA_WIDTH].astype(BF16)
        g_out = hg[:, 2 * GLA_QK + GLA_WIDTH:2 * GLA_QK + 2 * GLA_WIDTH]
        sg_ref[rows, :] = (g_out * _sigmoid(g_out)).astype(BF16)
        z = _dot(hg[:, GLA_GK_OFF:GLA_IN_PAD].astype(BF16), wgk_ref[...]) + bgk_ref[...]
        gk_ref[rows, :] = _log_sigmoid(z) * (1.0 / GLA_GATE_NORMALIZER)

        row = lax.broadcasted_iota(jnp.int32, (DENSE_SUB, 1), 0)
        prev = jnp.where(row == 0, last_row, pltpu.roll(hr, 1, 0))
        u = hr + (prev - hr) * mu_ref[...]
        r = u[:, 0:RWKV_WIDTH]
        k = u[:, RWKV_WIDTH:2 * RWKV_WIDTH]
        v = u[:, 2 * RWKV_WIDTH:3 * RWKV_WIDTH]
        low = u[:, LORA_OFF:LORA_OFF + LORA_IN]
        col = lax.broadcasted_iota(jnp.int32, (1, LORA_IN), 1)
        low = jnp.where(col < RWKV_W_LORA, jnp.tanh(low),
                        jnp.where(col < RWKV_W_LORA + RWKV_A_LORA, low, _sigmoid(low)))
        lo = _dot(low.astype(BF16), wlora_ref[...])
        lw_ref[rows, :] = -float(np.exp(-0.5)) * _sigmoid(w0_ref[...] + lo[:, 0:RWKV_WIDTH])
        a = _sigmoid(a0_ref[...] + lo[:, RWKV_WIDTH:2 * RWKV_WIDTH])
        g_ref[rows, :] = lo[:, 2 * RWKV_WIDTH:3 * RWKV_WIDTH].astype(BF16)

        ones = ones_ref[...]
        kk = k * kk_ref[...]
        ss = _dot((kk * kk).astype(BF16), ones)
        kkn = kk / jnp.maximum(jnp.sqrt(ss), L2_EPS)
        k2 = k * (1.0 + (a - 1.0) * ka_ref[...])
        bonus = _dot((r * k2 * rk_ref[...]).astype(BF16), ones) * v
        r_ref[rows, :] = r.astype(BF16)
        rk2_ref[rows, :] = k2.astype(BF16)
        rv_ref[rows, :] = v.astype(BF16)
        kkn_ref[rows, :] = kkn.astype(BF16)
        kb_ref[rows, :] = (kkn * a).astype(BF16)
        bonus_ref[rows, :] = bonus.astype(BF16)
        return hr[DENSE_SUB - 1:DENSE_SUB, :]

    last_row = carry_ref[...]
    pending = project(0)
    for j in range(n_sub):
        following = project(j + 1) if j + 1 < n_sub else None
        last_row = prepare(j, *pending, last_row)
        pending = following
    carry_ref[...] = last_row


def _gla_part(q_ref, k_ref, v_ref, sg_ref, gk_ref, nw_ref, tri_ref, ebig_ref, o_ref,
              s_ref, kp_ref, bp_ref, *, n_seq, n_chunk, seq_group):
    blk = n_seq * n_chunk * CHUNK
    heads = range(GLA_HEADS)
    subs = range(n_seq * n_chunk)
    hsl = [slice(h * GLA_DK, (h + 1) * GLA_DK) for h in heads]
    vsl = [slice(h * GLA_DV, (h + 1) * GLA_DV) for h in heads]
    rows = [slice(g * CHUNK, (g + 1) * CHUNK) for g in subs]
    n_pair = GLA_HEADS // 2
    ksl = [slice(p * 2 * GLA_DK, (p + 1) * 2 * GLA_DK) for p in range(n_pair)]
    wsl = [slice(p * 2 * GLA_DV, (p + 1) * 2 * GLA_DV) for p in range(n_pair)]
    chains = [(g, p) for g in subs for p in range(n_pair)]
    lane_k = lax.broadcasted_iota(jnp.int32, (1, GLA_QK), 1)
    lane_v = lax.broadcasted_iota(jnp.int32, (1, GLA_WIDTH), 1)
    first_k = (lane_k & GLA_DK) == 0
    first_v = (lane_v & GLA_DV) == 0
    first_s = lax.broadcasted_iota(jnp.int32, (GLA_DV, 2 * GLA_DK), 1) < GLA_DK
    lane_c = lax.broadcasted_iota(jnp.int32, (CHUNK, 2 * GLA_DK), 1)
    causal = lax.broadcasted_iota(jnp.int32, (CHUNK, 2 * GLA_DK), 0) >= (lane_c & (GLA_DK - 1))
    row_c = lax.broadcasted_iota(jnp.int32, (CHUNK, 1), 0)

    def load(ref):
        return ref[...].reshape(blk, ref.shape[-1])

    gk = load(gk_ref)
    n_rows = seq_group * CHUNK
    b = jnp.concatenate([_dot(tri_ref[...], jnp.concatenate(_split_bf16(gk[j * n_rows:(j + 1) * n_rows, :]), axis=0))
                         for j in range(blk // n_rows)], axis=0) * LOG2_E

    def finish(k, v_all, qe, score):
        b_last = {g: b[(g + 1) * CHUNK - 1:(g + 1) * CHUNK, :] for g in subs}
        kdec = {g: (k[rows[g], :] * jnp.exp2(b_last[g] - b[rows[g], :])).astype(BF16) for g in subs}
        kv = {(g, p): _dot_tn(v_all[rows[g], wsl[p]], kdec[g][:, ksl[p]]) for g, p in chains}
        new_state = {(s, p): s_ref[s, :, ksl[p]] for s in range(n_seq) for p in range(n_pair)}
        entering = {}
        for ck in range(n_chunk):
            for s in range(n_seq):
                g = s * n_chunk + ck
                for p in range(n_pair):
                    st = new_state[s, p]
                    entering[g, p] = jnp.concatenate([jnp.where(first_s, st, 0.0),
                                                      jnp.where(first_s, 0.0, st)], axis=0).astype(BF16)
                    new_state[s, p] = (st * jnp.exp2(b_last[g])[:, ksl[p]]
                                       + jnp.where(first_s, kv[g, p][0:GLA_DV], kv[g, p][GLA_DV:]))
        inter = {(g, p): _dot_nt(qe[rows[g], ksl[p]], entering[g, p]) for g, p in chains}
        sg = load(sg_ref)
        outs = []
        for g in subs:
            pieces = []
            for p in range(n_pair):
                op = inter[g, p] + score[g, p]
                for oh in (op[:, 0:GLA_DV], op[:, GLA_DV:]):
                    ms = jnp.mean(oh * oh, axis=-1, keepdims=True)
                    pieces.append(oh * lax.rsqrt(ms + GLA_NORM_EPS))
            o = jnp.concatenate(pieces, axis=-1) * nw_ref[...]
            outs.append((o * sg[rows[g], :].astype(F32)).astype(BF16))
        o_ref[...] = jnp.stack(outs, axis=0).reshape(n_seq, n_chunk * CHUNK, GLA_WIDTH)
        return new_state

    safe = jnp.min(b) >= -GLA_SAFE_LOG2

    q = load(q_ref).astype(F32)
    k = load(k_ref).astype(F32)
    v_all = load(v_ref)
    qe = (q * jnp.exp2(b)).astype(BF16)
    kn = (k * jnp.exp2(-b)).astype(BF16)
    kn_1, kn_2 = kn * jnp.where(first_k, 1.0, 0.0).astype(BF16), kn * jnp.where(first_k, 0.0, 1.0).astype(BF16)
    v_1, v_2 = v_all * jnp.where(first_v, 1.0, 0.0).astype(BF16), v_all * jnp.where(first_v, 0.0, 1.0).astype(BF16)
    raw = {(g, p): _dot_nt(qe[rows[g], ksl[p]],
                           jnp.concatenate([kn_1[rows[g], ksl[p]], kn_2[rows[g], ksl[p]]], axis=0))
           for g, p in chains}
    sc_fast = {ch: jnp.where(causal, raw[ch], 0.0).astype(BF16) for ch in chains}
    state_fast = finish(k, v_all, qe, {
        (g, p): _dot(sc_fast[g, p], jnp.concatenate([v_1[rows[g], wsl[p]], v_2[rows[g], wsl[p]]], axis=0))
        for g, p in chains})

    @pl.when(safe)
    def _():
        for s, p in state_fast:
            s_ref[s, :, ksl[p]] = state_fast[s, p]

    @pl.when(jnp.logical_not(safe))
    def _():
        kp_ref[0:SUB, :] = jnp.zeros((SUB, GLA_QK), F32)
        bp_ref[0:SUB, :] = jnp.zeros((SUB, GLA_QK), F32)
        kp_ref[SUB:SUB + blk, :] = k
        bp_ref[SUB:SUB + blk, :] = b
        rmod = lax.broadcasted_iota(jnp.int32, (blk, 1), 0) % SUB
        xs = [(q * k).astype(BF16)]
        for d in range(1, SUB):
            ks = kp_ref[SUB - d:SUB - d + blk, :]
            bs = bp_ref[SUB - d:SUB - d + blk, :]
            xs.append(jnp.where(rmod >= d, q * ks * jnp.exp2(b - bs), 0.0).astype(BF16))
        s_diag = _dot(jnp.concatenate(xs, axis=1), ebig_ref[...])
        lane = lax.broadcasted_iota(jnp.int32, (1, LANES), 1)
        score = {}
        for g in subs:
            bg = b[rows[g], :]
            for h in heads:
                qh, kh, bh = q[rows[g], hsl[h]], k[rows[g], hsl[h]], bg[:, hsl[h]]
                md = jnp.where((lane >= SUB * h) & (lane < SUB * (h + 1)), s_diag[rows[g], :], 0.0)
                sc_rows = [jnp.zeros((SUB, CHUNK), F32)]
                for sb in range(1, CHUNK // SUB):
                    lo_r = sb * SUB
                    ref = bh[lo_r - 1:lo_r, :]
                    qi = qh[lo_r:lo_r + SUB, :] * jnp.exp2(bh[lo_r:lo_r + SUB, :] - ref)
                    ki = jnp.where(row_c < lo_r, kh * jnp.exp2(jnp.minimum(ref - bh, 0.0)), 0.0)
                    sc_rows.append(_dot_nt(qi.astype(BF16), ki.astype(BF16)))
                sc = jnp.concatenate(sc_rows, axis=0) + pltpu.roll(
                    md, LANES - (SUB - 1) - SUB * h, 1, stride=1, stride_axis=0)[:, 0:CHUNK]
                score[g, h] = _dot(sc.astype(BF16), v_all[rows[g], vsl[h]])
        state_slow = finish(k, v_all, qe, {(g, p): jnp.concatenate([score[g, 2 * p], score[g, 2 * p + 1]], axis=1)
                                          for g, p in chains})
        for s, p in state_slow:
            s_ref[s, :, ksl[p]] = state_slow[s, p]


def _rwkv_part(r_ref, k_ref, v_ref, kk_ref, kb_ref, lw_ref, g_ref, bonus_ref, gnw_ref, gnb_ref,
               tri_ref, ones_ref, o_ref, s_ref, y_ref, *, n_seq, n_chunk, seq_group):
    n_pair = RWKV_HEADS // 2
    psl = [slice(p * RWKV_PAIR, (p + 1) * RWKV_PAIR) for p in range(n_pair)]
    subs = range(seq_group)
    rows = [slice(g * CHUNK, (g + 1) * CHUNK) for g in subs]
    chains = [(g, p) for g in subs for p in range(n_pair)]
    n_rows = seq_group * CHUNK

    lane_w = lax.broadcasted_iota(jnp.int32, (1, RWKV_WIDTH), 1)
    first_w = (lane_w & RWKV_HEAD) == 0

    keep_1 = jnp.where(first_w, 1.0, 0.0).astype(BF16)
    keep_2 = jnp.where(first_w, 0.0, 1.0).astype(BF16)

    def halves(a):
        return a * keep_1, a * keep_2

    lane_p = lax.broadcasted_iota(jnp.int32, (CHUNK, RWKV_PAIR), 1)
    row_p = lax.broadcasted_iota(jnp.int32, (CHUNK, RWKV_PAIR), 0)
    first_p = lane_p < RWKV_HEAD
    col_p = lane_p & (RWKV_HEAD - 1)
    strict = row_p > col_p
    incl = row_p >= col_p
    eye = (row_p == col_p).astype(F32)

    def bdiag(a):
        return jnp.concatenate([jnp.where(first_p, a, 0.0), jnp.where(first_p, 0.0, a)], axis=0).astype(BF16)

    def cut(a, ch):
        return a[rows[ch[0]], psl[ch[1]]]

    def prepare(ck, grp):
        def load(ref):
            return ref[grp * seq_group:(grp + 1) * seq_group, ck * CHUNK:(ck + 1) * CHUNK, :].reshape(
                n_rows, RWKV_WIDTH)

        lw = load(lw_ref)
        c = _dot(tri_ref[...], jnp.concatenate(_split_bf16(lw), axis=0))
        e_nc = jnp.exp(-c)
        bt_f = load(kb_ref).astype(F32) * e_nc
        kt_f = load(k_ref).astype(F32) * e_nc
        v_all = load(v_ref)
        at_all = (-load(kk_ref).astype(F32) * jnp.exp(c - lw)).astype(BF16)
        d = dict(v_all=v_all, at_all=at_all, rt_all=(load(r_ref).astype(F32) * jnp.exp(c)).astype(BF16))
        d['at_1'], d['at_2'] = halves(at_all)
        d['bt_1'], d['bt_2'] = halves(bt_f.astype(BF16))
        d['kt_1'], d['kt_2'] = halves(kt_f.astype(BF16))
        d['v_1'], d['v_2'] = halves(v_all)
        d['e_last'] = {g: jnp.exp(c[(g + 1) * CHUNK - 1:(g + 1) * CHUNK, :]) for g in subs}
        d['bh'] = {g: (bt_f[rows[g], :] * d['e_last'][g]).astype(BF16) for g in subs}
        d['kh'] = {g: (kt_f[rows[g], :] * d['e_last'][g]).astype(BF16) for g in subs}
        return d

    def first_level(d):
        return {ch: _dot_nt(jnp.concatenate([cut(d['at_all'], ch), cut(d['rt_all'], ch)], axis=0),
                            jnp.concatenate([cut(d['bt_1'], ch), cut(d['bt_2'], ch),
                                             cut(d['kt_1'], ch), cut(d['kt_2'], ch)], axis=0))
                for ch in chains}

    def run_chains(ck, grp, d, a_mat):
        trows = slice(ck * CHUNK, (ck + 1) * CHUNK)
        a_ab = {ch: jnp.where(strict, a_mat[ch][0:CHUNK, 0:RWKV_PAIR], 0.0) for ch in chains}
        a_kr = {ch: jnp.concatenate([jnp.where(strict, a_mat[ch][0:CHUNK, RWKV_PAIR:], 0.0),
                                     jnp.where(incl, a_mat[ch][CHUNK:, RWKV_PAIR:], 0.0)], axis=0).astype(BF16)
                for ch in chains}
        a_rb = {ch: jnp.where(incl, a_mat[ch][CHUNK:, 0:RWKV_PAIR], 0.0).astype(BF16) for ch in chains}
        av = {ch: _dot(a_kr[ch], jnp.concatenate([cut(d['v_1'], ch), cut(d['v_2'], ch)], axis=0)) for ch in chains}
        t_inv = {ch: eye + a_ab[ch] for ch in chains}
        pw = {ch: _dot(a_ab[ch].astype(BF16), bdiag(a_ab[ch])) for ch in chains}
        for _ in range(4):
            pt = {ch: _dot(jnp.concatenate([pw[ch], t_inv[ch]], axis=0).astype(BF16), bdiag(pw[ch]))
                  for ch in chains}
            pw = {ch: pt[ch][0:CHUNK] for ch in chains}
            t_inv = {ch: t_inv[ch] + pt[ch][CHUNK:] for ch in chains}
        t_inv = {ch: t_inv[ch] + _dot(t_inv[ch].astype(BF16), bdiag(pw[ch])) for ch in chains}
        wu = {ch: _dot(t_inv[ch].astype(BF16),
                       jnp.concatenate([jnp.concatenate([cut(d['at_1'], ch), cut(d['at_2'], ch)], axis=0),
                                        bdiag(av[ch][0:CHUNK])], axis=1)) for ch in chains}
        wr_til = {ch: jnp.concatenate([wu[ch][:, 0:RWKV_PAIR].astype(BF16), cut(d['rt_all'], ch)], axis=0)
                  for ch in chains}
        state = {(g, p): s_ref[grp * seq_group + g, :, psl[p]] for g, p in chains}
        ws = {ch: _dot_nt(wr_til[ch], bdiag(state[ch])) for ch in chains}
        u = {ch: ws[ch][0:CHUNK] + wu[ch][:, RWKV_PAIR:] for ch in chains}
        for g, p in chains:
            y_ref[grp * seq_group + g, trows, psl[p]] = (
                ws[g, p][CHUNK:] + _dot(a_rb[g, p], bdiag(u[g, p])) + av[g, p][CHUNK:])
        upd = {(g, p): _dot_tn(jnp.concatenate([u[g, p].astype(BF16), d['v_all'][rows[g], psl[p]]], axis=0),
                               jnp.concatenate([d['bh'][g][:, psl[p]], d['kh'][g][:, psl[p]]], axis=0))
               for g, p in chains}
        for g, p in chains:
            s_ref[grp * seq_group + g, :, psl[p]] = (state[g, p] * d['e_last'][g][:, psl[p]]
                                                     + jnp.where(first_p, upd[g, p][0:CHUNK], upd[g, p][CHUNK:]))

    def normalise(ck, grp):
        ones = ones_ref[...]
        gs = slice(grp * seq_group, (grp + 1) * seq_group)
        trows = slice(ck * CHUNK, (ck + 1) * CHUNK)
        y = y_ref[gs, trows, :].reshape(n_rows, RWKV_WIDTH)
        mu = _dot(y.astype(BF16), ones) * (1.0 / RWKV_HEAD)
        yc = y - mu
        var = _dot((yc * yc).astype(BF16), ones) * (1.0 / RWKV_HEAD)
        yn = yc * lax.rsqrt(var + RWKV_GN_EPS) * gnw_ref[...] + gnb_ref[...]
        o = ((yn + bonus_ref[gs, trows, :].reshape(n_rows, RWKV_WIDTH).astype(F32))
             * g_ref[gs, trows, :].reshape(n_rows, RWKV_WIDTH).astype(F32)).astype(BF16)
        o_ref[gs, trows, :] = o.reshape(seq_group, CHUNK, RWKV_WIDTH)

    steps = [(ck, grp) for ck in range(n_chunk) for grp in range(n_seq // seq_group)]
    pending = prepare(*steps[0])
    for i, (ck, grp) in enumerate(steps):
        following = prepare(*steps[i + 1]) if i + 1 < len(steps) else None
        run_chains(ck, grp, pending, first_level(pending))
        if i > 0:
            normalise(*steps[i - 1])
        pending = following
    normalise(*steps[-1])


def _mixers_kernel(gq_ref, gk_ref, gv_ref, gsg_ref, ggk_ref, gnw_ref, ebig_ref,
                   r_ref, k_ref, v_ref, kk_ref, kb_ref, lw_ref, g_ref, bonus_ref, rnw_ref, rnb_ref,
                   tri_ref, ones_ref, oa_ref, ob_ref,
                   gs_ref, kp_ref, bp_ref, rs_ref, y_ref, *, n_seq, n_chunk, seq_group):
    @pl.when(pl.program_id(1) == 0)
    def _():
        gs_ref[...] = jnp.zeros_like(gs_ref)
        rs_ref[...] = jnp.zeros_like(rs_ref)

    _rwkv_part(r_ref, k_ref, v_ref, kk_ref, kb_ref, lw_ref, g_ref, bonus_ref, rnw_ref, rnb_ref,
               tri_ref, ones_ref, ob_ref, rs_ref, y_ref, n_seq=n_seq, n_chunk=n_chunk, seq_group=seq_group)
    _gla_part(gq_ref, gk_ref, gv_ref, gsg_ref, ggk_ref, gnw_ref, tri_ref, ebig_ref, oa_ref,
              gs_ref, kp_ref, bp_ref, n_seq=n_seq, n_chunk=n_chunk, seq_group=seq_group)


def _layernorm(h, g, b):
    mu = jnp.mean(h, axis=-1, keepdims=True)
    hc = h - mu
    var = jnp.mean(hc * hc, axis=-1, keepdims=True)
    return hc * lax.rsqrt(var + LN_EPS) * g + b


def _mix_kernel(x_ref, oa_ref, ob_ref, wm_ref, bm_ref, wba_ref, wbb_ref, wout_ref, ln1g_ref, ln1b_ref,
                w1_ref, b1_ref, w2_ref, b2_ref, ln2g_ref, ln2b_ref, out_ref):
    n_sub = x_ref.shape[0] // DENSE_SUB

    def merge(j):
        rows = slice(j * DENSE_SUB, (j + 1) * DENSE_SUB)
        x = x_ref[rows, :]
        xb = x.astype(BF16)
        ya = _dot(oa_ref[rows, :], wba_ref[...])
        yb = _dot(ob_ref[rows, :], wbb_ref[...])
        ga = _sigmoid(_dot(xb, wm_ref[:, 0:D_MODEL]) + bm_ref[:, 0:D_MODEL])
        m = ga * ya
        gb = _sigmoid(_dot(xb, wm_ref[:, D_MODEL:]) + bm_ref[:, D_MODEL:])
        m = m + gb * yb
        return ALPHA * x + _dot(m.astype(BF16), wout_ref[...])

    def mlp(j, z):
        rows = slice(j * DENSE_SUB, (j + 1) * DENSE_SUB)
        x1 = _layernorm(z, ln1g_ref[...], ln1b_ref[...])
        x1b = x1.astype(BF16)
        acc = jnp.zeros_like(x1)
        for c in range(D_FF // FF_CHUNK):
            cs = slice(c * FF_CHUNK, (c + 1) * FF_CHUNK)
            h = jnp.maximum(_dot(x1b, w1_ref[:, cs]) + b1_ref[:, cs], 0.0)
            acc = acc + _dot((h * h).astype(BF16), w2_ref[cs, :])
        out_ref[rows, :] = _layernorm(ALPHA * x1 + acc + b2_ref[...], ln2g_ref[...], ln2b_ref[...])

    pending = merge(0)
    for j in range(n_sub):
        following = merge(j + 1) if j + 1 < n_sub else None
        mlp(j, pending)
        pending = following


def _block_ones(n, blk):
    idx = np.arange(n) // blk
    return jnp.asarray(idx[:, None] == idx[None, :], BF16)


def _layer(x2, bsz, seq, p, tm_proj, tm_mix):
    t = bsz * seq
    row = lambda a: a.reshape(1, -1).astype(F32)

    w_in = p['w_in']
    wg = jnp.pad(w_in[:, :GLA_IN], ((0, 0), (0, GLA_IN_PAD - GLA_IN))).astype(BF16)
    wr = w_in[:, GLA_IN:].astype(BF16)
    wgk = jnp.pad(p['w_gk_up'], ((0, LANES - GLA_GATE_RANK), (0, 0))).astype(BF16)
    a_off = RWKV_W_LORA + RWKV_A_LORA
    wlora = jnp.zeros((LORA_IN, 3 * RWKV_WIDTH), F32)
    wlora = wlora.at[0:RWKV_W_LORA, 0:RWKV_WIDTH].set(p['rwkv_w_up'])
    wlora = wlora.at[RWKV_W_LORA:a_off, RWKV_WIDTH:2 * RWKV_WIDTH].set(p['rwkv_a_up'])
    wlora = wlora.at[a_off:LORA_IN, 2 * RWKV_WIDTH:].set(p['rwkv_g_up']).astype(BF16)
    ones_r = _block_ones(RWKV_WIDTH, RWKV_HEAD)

    n_tiles = t // tm_proj
    tok = lambda w: pl.BlockSpec((tm_proj, w), lambda i: (i, 0))
    bshape = lambda w: jax.ShapeDtypeStruct((t, w), BF16)
    proj_out = pl.pallas_call(
        functools.partial(_proj_kernel, tiles_per_seq=seq // tm_proj),
        grid=(n_tiles,),
        in_specs=[tok(D_MODEL), _const_spec((D_MODEL, GLA_IN_PAD)), _const_spec((D_MODEL, RWKV_IN)),
                  _const_spec((1, RWKV_IN)), _const_spec((LANES, GLA_QK)), _const_spec((1, GLA_QK)),
                  _const_spec((LORA_IN, 3 * RWKV_WIDTH))] + [_const_spec((1, RWKV_WIDTH))] * 5
                 + [_const_spec((RWKV_WIDTH, RWKV_WIDTH))],
        out_specs=[tok(GLA_QK), tok(GLA_QK), tok(GLA_WIDTH), tok(GLA_WIDTH), tok(GLA_QK)]
                  + [tok(RWKV_WIDTH)] * 8,
        out_shape=[bshape(GLA_QK), bshape(GLA_QK), bshape(GLA_WIDTH), bshape(GLA_WIDTH),
                   jax.ShapeDtypeStruct((t, GLA_QK), F32)]
                  + [bshape(RWKV_WIDTH)] * 5 + [jax.ShapeDtypeStruct((t, RWKV_WIDTH), F32)]
                  + [bshape(RWKV_WIDTH)] * 2,
        scratch_shapes=[pltpu.VMEM((1, RWKV_IN), F32)],
        compiler_params=pltpu.CompilerParams(dimension_semantics=("arbitrary",),
                                             vmem_limit_bytes=VMEM_LIMIT_V7X),
        name="proj_prep",
    )(x2, wg, wr, row(p['mu_shift']), wgk, row(p['b_gk']), wlora, row(p['rwkv_w0']), row(p['rwkv_a0']),
      row(p['rwkv_k_k']), row(p['rwkv_k_a']), row(p['rwkv_r_k']), ones_r)
    gq, gk_, gv, gsg, ggk, rr, rk2, rv, rkk, rkb, rlw, rg, rbonus = proj_out

    def block_tri2(n_sub):
        tri = np.kron(np.eye(n_sub), np.tril(np.ones((CHUNK, CHUNK))))
        return jnp.asarray(np.concatenate([tri, tri], axis=1), BF16)

    rid = np.arange(SUB * GLA_QK)
    cid = SUB * ((rid % GLA_QK) // GLA_DK) + (SUB - 1) - rid // GLA_QK
    e_big = jnp.asarray(cid[:, None] == np.arange(LANES)[None, :], BF16)

    n_seq = max(n for n in (1, 2, 4, MIXER_SEQS) if bsz % n == 0)
    seq_group = min(MIXER_SEQ_GROUP, n_seq)
    n_chunk = MIXER_CHUNKS if seq % (MIXER_CHUNKS * CHUNK) == 0 else 1
    sspec = lambda w: pl.BlockSpec((n_seq, n_chunk * CHUNK, w), lambda b, n: (b, n, 0))
    seq3 = lambda a: a.reshape(bsz, seq, a.shape[-1])
    o_a, o_b = pl.pallas_call(
        functools.partial(_mixers_kernel, n_seq=n_seq, n_chunk=n_chunk, seq_group=seq_group),
        grid=(bsz // n_seq, seq // (n_chunk * CHUNK)),
        in_specs=[sspec(GLA_QK), sspec(GLA_QK), sspec(GLA_WIDTH), sspec(GLA_WIDTH), sspec(GLA_QK),
                  _const_spec((1, GLA_WIDTH)), _const_spec((SUB * GLA_QK, LANES))]
                 + [sspec(RWKV_WIDTH)] * 8 + [_const_spec((1, RWKV_WIDTH))] * 2
                 + [_const_spec((seq_group * CHUNK, 2 * seq_group * CHUNK)), _const_spec((RWKV_WIDTH, RWKV_WIDTH))],
        out_specs=[sspec(GLA_WIDTH), sspec(RWKV_WIDTH)],
        out_shape=[jax.ShapeDtypeStruct((bsz, seq, GLA_WIDTH), BF16),
                   jax.ShapeDtypeStruct((bsz, seq, RWKV_WIDTH), BF16)],
        scratch_shapes=[pltpu.VMEM((n_seq, GLA_DV, GLA_QK), F32),
                        pltpu.VMEM((SUB + n_seq * n_chunk * CHUNK, GLA_QK), F32),
                        pltpu.VMEM((SUB + n_seq * n_chunk * CHUNK, GLA_QK), F32),
                        pltpu.VMEM((n_seq, RWKV_HEAD, RWKV_WIDTH), F32),
                        pltpu.VMEM((n_seq, n_chunk * CHUNK, RWKV_WIDTH), F32)],
        compiler_params=pltpu.CompilerParams(dimension_semantics=("arbitrary", "arbitrary"),
                                             vmem_limit_bytes=VMEM_LIMIT_V7X),
        name="token_mixers",
    )(seq3(gq), seq3(gk_), seq3(gv), seq3(gsg), seq3(ggk), row(jnp.tile(p['gla_norm_w'], GLA_HEADS)), e_big,
      seq3(rr), seq3(rk2), seq3(rv), seq3(rkk), seq3(rkb), seq3(rlw), seq3(rg), seq3(rbonus),
      row(p['rwkv_gn_w']), row(p['rwkv_gn_b']), block_tri2(seq_group), ones_r)
    o_a = o_a.reshape(t, GLA_WIDTH)
    o_b = o_b.reshape(t, RWKV_WIDTH)

    mtok = lambda w: pl.BlockSpec((tm_mix, w), lambda i: (i, 0))
    out = pl.pallas_call(
        _mix_kernel,
        grid=(t // tm_mix,),
        in_specs=[mtok(D_MODEL), mtok(GLA_WIDTH), mtok(RWKV_WIDTH),
                  _const_spec((D_MODEL, 2 * D_MODEL)), _const_spec((1, 2 * D_MODEL)),
                  _const_spec((GLA_WIDTH, D_MODEL)), _const_spec((RWKV_WIDTH, D_MODEL)),
                  _const_spec((D_MODEL, D_MODEL)), _const_spec((1, D_MODEL)), _const_spec((1, D_MODEL)),
                  _const_spec((D_MODEL, D_FF)), _const_spec((1, D_FF)),
                  _const_spec((D_FF, D_MODEL)), _const_spec((1, D_MODEL)),
                  _const_spec((1, D_MODEL)), _const_spec((1, D_MODEL))],
        out_specs=mtok(D_MODEL),
        out_shape=jax.ShapeDtypeStruct((t, D_MODEL), F32),
        compiler_params=pltpu.CompilerParams(dimension_semantics=("arbitrary",),
                                             vmem_limit_bytes=VMEM_LIMIT_V7X),
        name="merge_mlp",
    )(x2, o_a, o_b, p['w_merge'].astype(BF16), row(p['b_merge']),
      p['w_branch'][0].astype(BF16), p['w_branch'][1].astype(BF16), p['w_out'].astype(BF16),
      row(p['ln1_g']), row(p['ln1_b']), p['w_mlp_up'].astype(BF16), row(p['b_mlp_up']),
      p['w_mlp_down'].astype(BF16), row(p['b_mlp_down']), row(p['ln2_g']), row(p['ln2_b']))
    return out


def kernel(x, w_in, mu_shift, w_gk_up, b_gk, gla_norm_w, rwkv_w0, rwkv_w_up, rwkv_a0, rwkv_a_up,
           rwkv_g_up, rwkv_k_k, rwkv_k_a, rwkv_r_k, rwkv_gn_w, rwkv_gn_b, w_merge, b_merge, w_branch,
           w_out, ln1_g, ln1_b, w_mlp_up, b_mlp_up, w_mlp_down, b_mlp_down, ln2_g, ln2_b):
    bsz, seq, d = x.shape
    assert d == D_MODEL and seq % DENSE_SUB == 0
    tm_proj = PROJ_TILE if seq % PROJ_TILE == 0 else DENSE_SUB
    tm_mix = MIX_TILE if (bsz * seq) % MIX_TILE == 0 else DENSE_SUB
    params = dict(w_in=w_in, mu_shift=mu_shift, w_gk_up=w_gk_up, b_gk=b_gk, gla_norm_w=gla_norm_w,
                  rwkv_w0=rwkv_w0, rwkv_w_up=rwkv_w_up, rwkv_a0=rwkv_a0, rwkv_a_up=rwkv_a_up,
                  rwkv_g_up=rwkv_g_up, rwkv_k_k=rwkv_k_k, rwkv_k_a=rwkv_k_a, rwkv_r_k=rwkv_r_k,
                  rwkv_gn_w=rwkv_gn_w, rwkv_gn_b=rwkv_gn_b, w_merge=w_merge, b_merge=b_merge,
                  w_branch=w_branch, w_out=w_out, ln1_g=ln1_g, ln1_b=ln1_b, w_mlp_up=w_mlp_up,
                  b_mlp_up=b_mlp_up, w_mlp_down=w_mlp_down, b_mlp_down=b_mlp_down, ln2_g=ln2_g, ln2_b=ln2_b)
    x2 = x.reshape(bsz * seq, d)
    for l in range(w_in.shape[0]):
        x2 = _layer(x2, bsz, seq, {n: a[l] for n, a in params.items()}, tm_proj, tm_mix)
    return x2.reshape(bsz, seq, d)
```

```python
import functools

import jax
import jax.numpy as jnp
import numpy as np
from jax import lax
from jax.experimental import pallas as pl
from jax.experimental.pallas import tpu as pltpu

F32 = jnp.float32
BF16 = jnp.bfloat16

D_MODEL = 1024
D_FF = 4 * D_MODEL
DEPTH = 1
CHUNK = 64
SUB = 16
GLA_HEADS, GLA_DK, GLA_DV = 4, 64, 128
GLA_GATE_RANK = 16
GLA_GATE_NORMALIZER = 16.0
GLA_NORM_EPS = 1e-5
RWKV_HEADS, RWKV_HEAD = 8, 64
RWKV_GN_EPS = 64e-5
L2_EPS = 1e-12
LN_EPS = 1e-5
ALPHA = (2.0 * DEPTH) ** 0.25
GLA_QK = GLA_HEADS * GLA_DK
GLA_WIDTH = GLA_HEADS * GLA_DV
RWKV_WIDTH = RWKV_HEADS * RWKV_HEAD
LANES = 128
GLA_GK_OFF = 2 * GLA_QK + 2 * GLA_WIDTH
GLA_IN = GLA_GK_OFF + GLA_GATE_RANK
GLA_IN_PAD = GLA_GK_OFF + LANES
RWKV_W_LORA, RWKV_A_LORA, RWKV_G_LORA = 64, 64, 128
LORA_OFF = 3 * RWKV_WIDTH
LORA_IN = RWKV_W_LORA + RWKV_A_LORA + RWKV_G_LORA
RWKV_IN = LORA_OFF + LORA_IN
RWKV_PAIR = 2 * RWKV_HEAD
GLA_PACK = 2 * GLA_QK + 2 * GLA_WIDTH
RWKV_PACK = 7 * RWKV_WIDTH
FF_CHUNK = 1024
DENSE_SUB = 256
PROJ_TILE = 1024
MIX_TILE = 512
MIXER_SEQS = 8
MIXER_SEQ_GROUP = 4
LOG2_E = float(np.log2(np.e))
GLA_SAFE_LOG2 = 96.0
VMEM_LIMIT_V7X = 56 * 1024 * 1024


def _dot(a, b):
    return jnp.dot(a, b, preferred_element_type=F32)


def _dot_nt(a, b):
    return lax.dot_general(a, b, (((1,), (1,)), ((), ())), preferred_element_type=F32)


def _dot_tn(a, b):
    return lax.dot_general(a, b, (((0,), (0,)), ((), ())), preferred_element_type=F32)


def _sigmoid(z):
    return 1.0 / (1.0 + jnp.exp(-z))


def _log_sigmoid(z):
    return jnp.minimum(z, 0.0) - jnp.log(1.0 + jnp.exp(-jnp.abs(z)))


def _split_bf16(a):
    hi = a.astype(BF16)
    lo = (a - hi.astype(F32)).astype(BF16)
    return hi, lo


def _const_spec(shape):
    nd = len(shape)
    return pl.BlockSpec(shape, lambda *_: (0,) * nd, pipeline_mode=pl.Buffered(1))


def _proj_kernel(x_ref, wg_ref, wr_ref, mu_ref, wgk_ref, bgk_ref, wlora_ref, w0_ref, a0_ref,
                 kk_ref, ka_ref, rk_ref, ones_ref,
                 ga_ref, rw_ref, lg_ref,
                 carry_ref, *, tiles_per_seq):
    def put(ref, rows, n, width, val):
        ref[rows, n * width:(n + 1) * width] = val.astype(BF16)

    i = pl.program_id(0)
    n_sub = x_ref.shape[0] // DENSE_SUB

    @pl.when(i % tiles_per_seq == 0)
    def _():
        carry_ref[...] = jnp.zeros_like(carry_ref)

    def project(j):
        xb = x_ref[j * DENSE_SUB:(j + 1) * DENSE_SUB, :].astype(BF16)
        return _dot(xb, wg_ref[...]), _dot(xb, wr_ref[...])

    def prepare(j, hg, hr, last_row):
        rows = slice(j * DENSE_SUB, (j + 1) * DENSE_SUB)
        put(ga_ref, rows, 0, GLA_QK, hg[:, 0:GLA_QK] * (GLA_DK ** -0.5))
        put(ga_ref, rows, 1, GLA_QK, hg[:, GLA_QK:2 * GLA_QK])
        put(ga_ref, rows, 1, GLA_WIDTH, hg[:, 2 * GLA_QK:2 * GLA_QK + GLA_WIDTH])
        g_out = hg[:, 2 * GLA_QK + GLA_WIDTH:2 * GLA_QK + 2 * GLA_WIDTH]
        put(ga_ref, rows, 2, GLA_WIDTH, g_out * _sigmoid(g_out))
        z = _dot(hg[:, GLA_GK_OFF:GLA_IN_PAD].astype(BF16), wgk_ref[...]) + bgk_ref[...]
        lg_ref[rows, 0:GLA_QK] = _log_sigmoid(z) * (1.0 / GLA_GATE_NORMALIZER)

        row = lax.broadcasted_iota(jnp.int32, (DENSE_SUB, 1), 0)
        prev = jnp.where(row == 0, last_row, pltpu.roll(hr, 1, 0))
        u = hr + (prev - hr) * mu_ref[...]
        r = u[:, 0:RWKV_WIDTH]
        k = u[:, RWKV_WIDTH:2 * RWKV_WIDTH]
        v = u[:, 2 * RWKV_WIDTH:3 * RWKV_WIDTH]
        low = u[:, LORA_OFF:LORA_OFF + LORA_IN]
        col = lax.broadcasted_iota(jnp.int32, (1, LORA_IN), 1)
        low = jnp.where(col < RWKV_W_LORA, jnp.tanh(low),
                        jnp.where(col < RWKV_W_LORA + RWKV_A_LORA, low, _sigmoid(low)))
        lo = _dot(low.astype(BF16), wlora_ref[...])
        lg_ref[rows, GLA_QK:] = -float(np.exp(-0.5)) * _sigmoid(w0_ref[...] + lo[:, 0:RWKV_WIDTH])
        a = _sigmoid(a0_ref[...] + lo[:, RWKV_WIDTH:2 * RWKV_WIDTH])
        put(rw_ref, rows, 5, RWKV_WIDTH, lo[:, 2 * RWKV_WIDTH:3 * RWKV_WIDTH])

        ones = ones_ref[...]
        kk = k * kk_ref[...]
        ss = _dot((kk * kk).astype(BF16), ones)
        kkn = kk / jnp.maximum(jnp.sqrt(ss), L2_EPS)
        k2 = k * (1.0 + (a - 1.0) * ka_ref[...])
        bonus = _dot((r * k2 * rk_ref[...]).astype(BF16), ones) * v
        put(rw_ref, rows, 0, RWKV_WIDTH, r)
        put(rw_ref, rows, 1, RWKV_WIDTH, k2)
        put(rw_ref, rows, 2, RWKV_WIDTH, v)
        put(rw_ref, rows, 3, RWKV_WIDTH, kkn)
        put(rw_ref, rows, 4, RWKV_WIDTH, kkn * a)
        put(rw_ref, rows, 6, RWKV_WIDTH, bonus)
        return hr[DENSE_SUB - 1:DENSE_SUB, :]

    last_row = carry_ref[...]
    pending = project(0)
    for j in range(n_sub):
        following = project(j + 1) if j + 1 < n_sub else None
        last_row = prepare(j, *pending, last_row)
        pending = following
    carry_ref[...] = last_row


def _gla_part(q_ref, k_ref, v_ref, sg_ref, gk_ref, nw_ref, tri_ref, ebig_ref, o_ref,
              s_ref, kp_ref, bp_ref, *, n_seq, seq_group):
    blk = n_seq * CHUNK
    heads = range(GLA_HEADS)
    subs = range(n_seq)
    hsl = [slice(h * GLA_DK, (h + 1) * GLA_DK) for h in heads]
    vsl = [slice(h * GLA_DV, (h + 1) * GLA_DV) for h in heads]
    rows = [slice(g * CHUNK, (g + 1) * CHUNK) for g in subs]
    n_pair = GLA_HEADS // 2
    ksl = [slice(p * 2 * GLA_DK, (p + 1) * 2 * GLA_DK) for p in range(n_pair)]
    wsl = [slice(p * 2 * GLA_DV, (p + 1) * 2 * GLA_DV) for p in range(n_pair)]
    chains = [(g, p) for g in subs for p in range(n_pair)]
    lane_k = lax.broadcasted_iota(jnp.int32, (1, GLA_QK), 1)
    lane_v = lax.broadcasted_iota(jnp.int32, (1, GLA_WIDTH), 1)
    first_k = (lane_k & GLA_DK) == 0
    first_v = (lane_v & GLA_DV) == 0
    first_s = lax.broadcasted_iota(jnp.int32, (GLA_DV, 2 * GLA_DK), 1) < GLA_DK
    lane_c = lax.broadcasted_iota(jnp.int32, (CHUNK, 2 * GLA_DK), 1)
    causal = lax.broadcasted_iota(jnp.int32, (CHUNK, 2 * GLA_DK), 0) >= (lane_c & (GLA_DK - 1))
    row_c = lax.broadcasted_iota(jnp.int32, (CHUNK, 1), 0)

    def load(ref):
        return ref[...].reshape(blk, ref.shape[-1])

    gk = load(gk_ref)
    n_rows = seq_group * CHUNK
    b = jnp.concatenate([_dot(tri_ref[...], jnp.concatenate(_split_bf16(gk[j * n_rows:(j + 1) * n_rows, :]), axis=0))
                         for j in range(n_seq // seq_group)], axis=0) * LOG2_E

    def finish(k, v_all, qe, score):
        b_last = {g: b[(g + 1) * CHUNK - 1:(g + 1) * CHUNK, :] for g in subs}
        kdec = {g: (k[rows[g], :] * jnp.exp2(b_last[g] - b[rows[g], :])).astype(BF16) for g in subs}
        kv = {(g, p): _dot_tn(v_all[rows[g], wsl[p]], kdec[g][:, ksl[p]]) for g, p in chains}
        state = {(g, p): s_ref[g, :, ksl[p]] for g, p in chains}
        entering = {ch: jnp.concatenate([jnp.where(first_s, state[ch], 0.0),
                                         jnp.where(first_s, 0.0, state[ch])], axis=0).astype(BF16) for ch in chains}
        new_state = {(g, p): state[g, p] * jnp.exp2(b_last[g])[:, ksl[p]]
                     + jnp.where(first_s, kv[g, p][0:GLA_DV], kv[g, p][GLA_DV:]) for g, p in chains}
        inter = {(g, p): _dot_nt(qe[rows[g], ksl[p]], entering[g, p]) for g, p in chains}
        sg = load(sg_ref)
        outs = []
        for g in subs:
            pieces = []
            for p in range(n_pair):
                op = inter[g, p] + score[g, p]
                for oh in (op[:, 0:GLA_DV], op[:, GLA_DV:]):
                    ms = jnp.mean(oh * oh, axis=-1, keepdims=True)
                    pieces.append(oh * lax.rsqrt(ms + GLA_NORM_EPS))
            o = jnp.concatenate(pieces, axis=-1) * nw_ref[...]
            outs.append((o * sg[rows[g], :].astype(F32)).astype(BF16))
        o_ref[...] = jnp.stack(outs, axis=0)
        return new_state

    safe = jnp.min(b) >= -GLA_SAFE_LOG2

    q = load(q_ref).astype(F32)
    k = load(k_ref).astype(F32)
    v_all = load(v_ref)
    qe = (q * jnp.exp2(b)).astype(BF16)
    kn = (k * jnp.exp2(-b)).astype(BF16)
    kn_1, kn_2 = kn * jnp.where(first_k, 1.0, 0.0).astype(BF16), kn * jnp.where(first_k, 0.0, 1.0).astype(BF16)
    v_1, v_2 = v_all * jnp.where(first_v, 1.0, 0.0).astype(BF16), v_all * jnp.where(first_v, 0.0, 1.0).astype(BF16)
    raw = {(g, p): _dot_nt(qe[rows[g], ksl[p]],
                           jnp.concatenate([kn_1[rows[g], ksl[p]], kn_2[rows[g], ksl[p]]], axis=0))
           for g, p in chains}
    sc_fast = {ch: jnp.where(causal, raw[ch], 0.0).astype(BF16) for ch in chains}
    state_fast = finish(k, v_all, qe, {
        (g, p): _dot(sc_fast[g, p], jnp.concatenate([v_1[rows[g], wsl[p]], v_2[rows[g], wsl[p]]], axis=0))
        for g, p in chains})

    @pl.when(safe)
    def _():
        for g, p in chains:
            s_ref[g, :, ksl[p]] = state_fast[g, p]

    @pl.when(jnp.logical_not(safe))
    def _():
        kp_ref[0:SUB, :] = jnp.zeros((SUB, GLA_QK), F32)
        bp_ref[0:SUB, :] = jnp.zeros((SUB, GLA_QK), F32)
        kp_ref[SUB:SUB + blk, :] = k
        bp_ref[SUB:SUB + blk, :] = b
        rmod = lax.broadcasted_iota(jnp.int32, (blk, 1), 0) % SUB
        xs = [(q * k).astype(BF16)]
        for d in range(1, SUB):
            ks = kp_ref[SUB - d:SUB - d + blk, :]
            bs = bp_ref[SUB - d:SUB - d + blk, :]
            xs.append(jnp.where(rmod >= d, q * ks * jnp.exp2(b - bs), 0.0).astype(BF16))
        s_diag = _dot(jnp.concatenate(xs, axis=1), ebig_ref[...])
        lane = lax.broadcasted_iota(jnp.int32, (1, LANES), 1)
        score = {}
        for g in subs:
            bg = b[rows[g], :]
            for h in heads:
                qh, kh, bh = q[rows[g], hsl[h]], k[rows[g], hsl[h]], bg[:, hsl[h]]
                md = jnp.where((lane >= SUB * h) & (lane < SUB * (h + 1)), s_diag[rows[g], :], 0.0)
                sc_rows = [jnp.zeros((SUB, CHUNK), F32)]
                for sb in range(1, CHUNK // SUB):
                    lo_r = sb * SUB
                    ref = bh[lo_r - 1:lo_r, :]
                    qi = qh[lo_r:lo_r + SUB, :] * jnp.exp2(bh[lo_r:lo_r + SUB, :] - ref)
                    ki = jnp.where(row_c < lo_r, kh * jnp.exp2(jnp.minimum(ref - bh, 0.0)), 0.0)
                    sc_rows.append(_dot_nt(qi.astype(BF16), ki.astype(BF16)))
                sc = jnp.concatenate(sc_rows, axis=0) + pltpu.roll(
                    md, LANES - (SUB - 1) - SUB * h, 1, stride=1, stride_axis=0)[:, 0:CHUNK]
                score[g, h] = _dot(sc.astype(BF16), v_all[rows[g], vsl[h]])
        state_slow = finish(k, v_all, qe, {(g, p): jnp.concatenate([score[g, 2 * p], score[g, 2 * p + 1]], axis=1)
                                          for g, p in chains})
        for g, p in chains:
            s_ref[g, :, ksl[p]] = state_slow[g, p]


def _rwkv_part(r_ref, k_ref, v_ref, kk_ref, kb_ref, lw_ref, g_ref, bonus_ref, gnw_ref, gnb_ref,
               tri_ref, ones_ref, o_ref, s_ref, y_ref, *, n_seq, seq_group):
    n_pair = RWKV_HEADS // 2
    psl = [slice(p * RWKV_PAIR, (p + 1) * RWKV_PAIR) for p in range(n_pair)]
    subs = range(seq_group)
    rows = [slice(g * CHUNK, (g + 1) * CHUNK) for g in subs]
    chains = [(g, p) for g in subs for p in range(n_pair)]
    n_rows = seq_group * CHUNK

    lane_w = lax.broadcasted_iota(jnp.int32, (1, RWKV_WIDTH), 1)
    first_w = (lane_w & RWKV_HEAD) == 0

    keep_1 = jnp.where(first_w, 1.0, 0.0).astype(BF16)
    keep_2 = jnp.where(first_w, 0.0, 1.0).astype(BF16)

    def halves(a):
        return a * keep_1, a * keep_2

    lane_p = lax.broadcasted_iota(jnp.int32, (CHUNK, RWKV_PAIR), 1)
    row_p = lax.broadcasted_iota(jnp.int32, (CHUNK, RWKV_PAIR), 0)
    first_p = lane_p < RWKV_HEAD
    col_p = lane_p & (RWKV_HEAD - 1)
    strict = row_p > col_p
    incl = row_p >= col_p
    eye = (row_p == col_p).astype(F32)

    def bdiag(a):
        return jnp.concatenate([jnp.where(first_p, a, 0.0), jnp.where(first_p, 0.0, a)], axis=0).astype(BF16)

    def cut(a, ch):
        return a[rows[ch[0]], psl[ch[1]]]

    def prepare(grp):
        def load(ref):
            return ref[grp * seq_group:(grp + 1) * seq_group].reshape(n_rows, RWKV_WIDTH)

        lw = load(lw_ref)
        c = _dot(tri_ref[...], jnp.concatenate(_split_bf16(lw), axis=0))
        e_nc = jnp.exp(-c)
        bt_f = load(kb_ref).astype(F32) * e_nc
        kt_f = load(k_ref).astype(F32) * e_nc
        v_all = load(v_ref)
        at_all = (-load(kk_ref).astype(F32) * jnp.exp(c - lw)).astype(BF16)
        d = dict(v_all=v_all, at_all=at_all, rt_all=(load(r_ref).astype(F32) * jnp.exp(c)).astype(BF16))
        d['at_1'], d['at_2'] = halves(at_all)
        d['bt_1'], d['bt_2'] = halves(bt_f.astype(BF16))
        d['kt_1'], d['kt_2'] = halves(kt_f.astype(BF16))
        d['v_1'], d['v_2'] = halves(v_all)
        d['e_last'] = {g: jnp.exp(c[(g + 1) * CHUNK - 1:(g + 1) * CHUNK, :]) for g in subs}
        d['bh'] = {g: (bt_f[rows[g], :] * d['e_last'][g]).astype(BF16) for g in subs}
        d['kh'] = {g: (kt_f[rows[g], :] * d['e_last'][g]).astype(BF16) for g in subs}
        return d

    def first_level(d):
        return {ch: _dot_nt(jnp.concatenate([cut(d['at_all'], ch), cut(d['rt_all'], ch)], axis=0),
                            jnp.concatenate([cut(d['bt_1'], ch), cut(d['bt_2'], ch),
                                             cut(d['kt_1'], ch), cut(d['kt_2'], ch)], axis=0))
                for ch in chains}

    def run_chains(grp, d, a_mat):
        a_ab = {ch: jnp.where(strict, a_mat[ch][0:CHUNK, 0:RWKV_PAIR], 0.0) for ch in chains}
        a_kr = {ch: jnp.concatenate([jnp.where(strict, a_mat[ch][0:CHUNK, RWKV_PAIR:], 0.0),
                                     jnp.where(incl, a_mat[ch][CHUNK:, RWKV_PAIR:], 0.0)], axis=0).astype(BF16)
                for ch in chains}
        a_rb = {ch: jnp.where(incl, a_mat[ch][CHUNK:, 0:RWKV_PAIR], 0.0).astype(BF16) for ch in chains}
        av = {ch: _dot(a_kr[ch], jnp.concatenate([cut(d['v_1'], ch), cut(d['v_2'], ch)], axis=0)) for ch in chains}
        t_inv = {ch: eye + a_ab[ch] for ch in chains}
        pw = {ch: _dot(a_ab[ch].astype(BF16), bdiag(a_ab[ch])) for ch in chains}
        for _ in range(4):
            pt = {ch: _dot(jnp.concatenate([pw[ch], t_inv[ch]], axis=0).astype(BF16), bdiag(pw[ch]))
                  for ch in chains}
            pw = {ch: pt[ch][0:CHUNK] for ch in chains}
            t_inv = {ch: t_inv[ch] + pt[ch][CHUNK:] for ch in chains}
        t_inv = {ch: t_inv[ch] + _dot(t_inv[ch].astype(BF16), bdiag(pw[ch])) for ch in chains}
        wu = {ch: _dot(t_inv[ch].astype(BF16),
                       jnp.concatenate([jnp.concatenate([cut(d['at_1'], ch), cut(d['at_2'], ch)], axis=0),
                                        bdiag(av[ch][0:CHUNK])], axis=1)) for ch in chains}
        wr_til = {ch: jnp.concatenate([wu[ch][:, 0:RWKV_PAIR].astype(BF16), cut(d['rt_all'], ch)], axis=0)
                  for ch in chains}
        state = {(g, p): s_ref[grp * seq_group + g, :, psl[p]] for g, p in chains}
        ws = {ch: _dot_nt(wr_til[ch], bdiag(state[ch])) for ch in chains}
        u = {ch: ws[ch][0:CHUNK] + wu[ch][:, RWKV_PAIR:] for ch in chains}
        for g, p in chains:
            y_ref[grp * n_rows + g * CHUNK:grp * n_rows + (g + 1) * CHUNK, psl[p]] = (
                ws[g, p][CHUNK:] + _dot(a_rb[g, p], bdiag(u[g, p])) + av[g, p][CHUNK:])
        upd = {(g, p): _dot_tn(jnp.concatenate([u[g, p].astype(BF16), d['v_all'][rows[g], psl[p]]], axis=0),
                               jnp.concatenate([d['bh'][g][:, psl[p]], d['kh'][g][:, psl[p]]], axis=0))
               for g, p in chains}
        for g, p in chains:
            s_ref[grp * seq_group + g, :, psl[p]] = (state[g, p] * d['e_last'][g][:, psl[p]]
                                                     + jnp.where(first_p, upd[g, p][0:CHUNK], upd[g, p][CHUNK:]))

    def normalise(grp):
        ones = ones_ref[...]
        y = y_ref[grp * n_rows:(grp + 1) * n_rows, :]
        mu = _dot(y.astype(BF16), ones) * (1.0 / RWKV_HEAD)
        yc = y - mu
        var = _dot((yc * yc).astype(BF16), ones) * (1.0 / RWKV_HEAD)
        yn = yc * lax.rsqrt(var + RWKV_GN_EPS) * gnw_ref[...] + gnb_ref[...]
        gs = slice(grp * seq_group, (grp + 1) * seq_group)
        o = ((yn + bonus_ref[gs].reshape(n_rows, RWKV_WIDTH).astype(F32))
             * g_ref[gs].reshape(n_rows, RWKV_WIDTH).astype(F32)).astype(BF16)
        o_ref[gs] = o.reshape(seq_group, CHUNK, RWKV_WIDTH)

    n_grp = n_seq // seq_group
    pending = prepare(0)
    for grp in range(n_grp):
        following = prepare(grp + 1) if grp + 1 < n_grp else None
        run_chains(grp, pending, first_level(pending))
        if grp > 0:
            normalise(grp - 1)
        pending = following
    normalise(n_grp - 1)


def _mixers_kernel(ga_ref, rw_ref, lg_ref, gnw_ref, ebig_ref, rnw_ref, rnb_ref,
                   tri_ref, ones_ref, oab_ref,
                   gs_ref, kp_ref, bp_ref, rs_ref, y_ref, *, n_seq, seq_group):
    oa_ref = oab_ref.at[:, :, 0:GLA_WIDTH]
    ob_ref = oab_ref.at[:, :, GLA_WIDTH:GLA_WIDTH + RWKV_WIDTH]
    gq_ref, gk_ref = (ga_ref.at[:, :, n * GLA_QK:(n + 1) * GLA_QK] for n in range(2))
    gv_ref, gsg_ref = (ga_ref.at[:, :, n * GLA_WIDTH:(n + 1) * GLA_WIDTH] for n in range(1, 3))
    r_ref, k_ref, v_ref, kk_ref, kb_ref, g_ref, bonus_ref = (
        rw_ref.at[:, :, n * RWKV_WIDTH:(n + 1) * RWKV_WIDTH] for n in range(7))
    ggk_ref = lg_ref.at[:, :, 0:GLA_QK]
    lw_ref = lg_ref.at[:, :, GLA_QK:GLA_QK + RWKV_WIDTH]
    @pl.when(pl.program_id(1) == 0)
    def _():
        gs_ref[...] = jnp.zeros_like(gs_ref)
        rs_ref[...] = jnp.zeros_like(rs_ref)

    _rwkv_part(r_ref, k_ref, v_ref, kk_ref, kb_ref, lw_ref, g_ref, bonus_ref, rnw_ref, rnb_ref,
               tri_ref, ones_ref, ob_ref, rs_ref, y_ref, n_seq=n_seq, seq_group=seq_group)
    _gla_part(gq_ref, gk_ref, gv_ref, gsg_ref, ggk_ref, gnw_ref, tri_ref, ebig_ref, oa_ref,
              gs_ref, kp_ref, bp_ref, n_seq=n_seq, seq_group=seq_group)


def _layernorm(h, g, b):
    mu = jnp.mean(h, axis=-1, keepdims=True)
    hc = h - mu
    var = jnp.mean(hc * hc, axis=-1, keepdims=True)
    return hc * lax.rsqrt(var + LN_EPS) * g + b


def _mix_kernel(x_ref, oab_ref, wm_ref, bm_ref, wba_ref, wbb_ref, wout_ref, ln1g_ref, ln1b_ref,
                w1_ref, b1_ref, w2_ref, b2_ref, ln2g_ref, ln2b_ref, out_ref):
    n_sub = x_ref.shape[0] // DENSE_SUB

    def merge(j):
        rows = slice(j * DENSE_SUB, (j + 1) * DENSE_SUB)
        x = x_ref[rows, :]
        xb = x.astype(BF16)
        ya = _dot(oab_ref[rows, 0:GLA_WIDTH], wba_ref[...])
        yb = _dot(oab_ref[rows, GLA_WIDTH:GLA_WIDTH + RWKV_WIDTH], wbb_ref[...])
        ga = _sigmoid(_dot(xb, wm_ref[:, 0:D_MODEL]) + bm_ref[:, 0:D_MODEL])
        m = ga * ya
        gb = _sigmoid(_dot(xb, wm_ref[:, D_MODEL:]) + bm_ref[:, D_MODEL:])
        m = m + gb * yb
        return ALPHA * x + _dot(m.astype(BF16), wout_ref[...])

    def mlp(j, z):
        rows = slice(j * DENSE_SUB, (j + 1) * DENSE_SUB)
        x1 = _layernorm(z, ln1g_ref[...], ln1b_ref[...])
        x1b = x1.astype(BF16)
        acc = jnp.zeros_like(x1)
        for c in range(D_FF // FF_CHUNK):
            cs = slice(c * FF_CHUNK, (c + 1) * FF_CHUNK)
            h = jnp.maximum(_dot(x1b, w1_ref[:, cs]) + b1_ref[:, cs], 0.0)
            acc = acc + _dot((h * h).astype(BF16), w2_ref[cs, :])
        out_ref[rows, :] = _layernorm(ALPHA * x1 + acc + b2_ref[...], ln2g_ref[...], ln2b_ref[...])

    pending = merge(0)
    for j in range(n_sub):
        following = merge(j + 1) if j + 1 < n_sub else None
        mlp(j, pending)
        pending = following


def _block_ones(n, blk):
    idx = np.arange(n) // blk
    return jnp.asarray(idx[:, None] == idx[None, :], BF16)


def _layer(x2, bsz, seq, p, tm_proj, tm_mix):
    t = bsz * seq
    row = lambda a: a.reshape(1, -1).astype(F32)

    w_in = p['w_in']
    wg = jnp.pad(w_in[:, :GLA_IN], ((0, 0), (0, GLA_IN_PAD - GLA_IN))).astype(BF16)
    wr = w_in[:, GLA_IN:].astype(BF16)
    wgk = jnp.pad(p['w_gk_up'], ((0, LANES - GLA_GATE_RANK), (0, 0))).astype(BF16)
    a_off = RWKV_W_LORA + RWKV_A_LORA
    wlora = jnp.zeros((LORA_IN, 3 * RWKV_WIDTH), F32)
    wlora = wlora.at[0:RWKV_W_LORA, 0:RWKV_WIDTH].set(p['rwkv_w_up'])
    wlora = wlora.at[RWKV_W_LORA:a_off, RWKV_WIDTH:2 * RWKV_WIDTH].set(p['rwkv_a_up'])
    wlora = wlora.at[a_off:LORA_IN, 2 * RWKV_WIDTH:].set(p['rwkv_g_up']).astype(BF16)
    ones_r = _block_ones(RWKV_WIDTH, RWKV_HEAD)

    n_tiles = t // tm_proj
    tok = lambda w: pl.BlockSpec((tm_proj, w), lambda i: (i, 0))
    bshape = lambda w: jax.ShapeDtypeStruct((t, w), BF16)
    proj_out = pl.pallas_call(
        functools.partial(_proj_kernel, tiles_per_seq=seq // tm_proj),
        grid=(n_tiles,),
        in_specs=[tok(D_MODEL), _const_spec((D_MODEL, GLA_IN_PAD)), _const_spec((D_MODEL, RWKV_IN)),
                  _const_spec((1, RWKV_IN)), _const_spec((LANES, GLA_QK)), _const_spec((1, GLA_QK)),
                  _const_spec((LORA_IN, 3 * RWKV_WIDTH))] + [_const_spec((1, RWKV_WIDTH))] * 5
                 + [_const_spec((RWKV_WIDTH, RWKV_WIDTH))],
        out_specs=[tok(GLA_PACK), tok(RWKV_PACK), tok(GLA_QK + RWKV_WIDTH)],
        out_shape=[bshape(GLA_PACK), bshape(RWKV_PACK), jax.ShapeDtypeStruct((t, GLA_QK + RWKV_WIDTH), F32)],
        scratch_shapes=[pltpu.VMEM((1, RWKV_IN), F32)],
        compiler_params=pltpu.CompilerParams(dimension_semantics=("arbitrary",),
                                             vmem_limit_bytes=VMEM_LIMIT_V7X),
        name="proj_prep",
    )(x2, wg, wr, row(p['mu_shift']), wgk, row(p['b_gk']), wlora, row(p['rwkv_w0']), row(p['rwkv_a0']),
      row(p['rwkv_k_k']), row(p['rwkv_k_a']), row(p['rwkv_r_k']), ones_r)
    g_pack, r_pack, log_gates = proj_out

    def block_tri2(n_sub):
        tri = np.kron(np.eye(n_sub), np.tril(np.ones((CHUNK, CHUNK))))
        return jnp.asarray(np.concatenate([tri, tri], axis=1), BF16)

    rid = np.arange(SUB * GLA_QK)
    cid = SUB * ((rid % GLA_QK) // GLA_DK) + (SUB - 1) - rid // GLA_QK
    e_big = jnp.asarray(cid[:, None] == np.arange(LANES)[None, :], BF16)

    n_seq = max(n for n in (1, 2, 4, MIXER_SEQS) if bsz % n == 0)
    seq_group = min(MIXER_SEQ_GROUP, n_seq)
    sspec = lambda w: pl.BlockSpec((n_seq, CHUNK, w), lambda b, n: (b, n, 0))
    seq3 = lambda a: a.reshape(bsz, seq, a.shape[-1])
    o_ab = pl.pallas_call(
        functools.partial(_mixers_kernel, n_seq=n_seq, seq_group=seq_group),
        grid=(bsz // n_seq, seq // CHUNK),
        in_specs=[sspec(GLA_PACK), sspec(RWKV_PACK), sspec(GLA_QK + RWKV_WIDTH), _const_spec((1, GLA_WIDTH)),
                  _const_spec((SUB * GLA_QK, LANES))] + [_const_spec((1, RWKV_WIDTH))] * 2
                 + [_const_spec((seq_group * CHUNK, 2 * seq_group * CHUNK)), _const_spec((RWKV_WIDTH, RWKV_WIDTH))],
        out_specs=sspec(GLA_WIDTH + RWKV_WIDTH),
        out_shape=jax.ShapeDtypeStruct((bsz, seq, GLA_WIDTH + RWKV_WIDTH), BF16),
        scratch_shapes=[pltpu.VMEM((n_seq, GLA_DV, GLA_QK), F32),
                        pltpu.VMEM((SUB + n_seq * CHUNK, GLA_QK), F32),
                        pltpu.VMEM((SUB + n_seq * CHUNK, GLA_QK), F32),
                        pltpu.VMEM((n_seq, RWKV_HEAD, RWKV_WIDTH), F32),
                        pltpu.VMEM((n_seq * CHUNK, RWKV_WIDTH), F32)],
        compiler_params=pltpu.CompilerParams(dimension_semantics=("arbitrary", "arbitrary"),
                                             vmem_limit_bytes=VMEM_LIMIT_V7X),
        name="token_mixers",
    )(seq3(g_pack), seq3(r_pack), seq3(log_gates), row(jnp.tile(p['gla_norm_w'], GLA_HEADS)), e_big,
      row(p['rwkv_gn_w']), row(p['rwkv_gn_b']), block_tri2(seq_group), ones_r)
    o_ab = o_ab.reshape(t, GLA_WIDTH + RWKV_WIDTH)

    mtok = lambda w: pl.BlockSpec((tm_mix, w), lambda i: (i, 0))
    out = pl.pallas_call(
        _mix_kernel,
        grid=(t // tm_mix,),
        in_specs=[mtok(D_MODEL), mtok(GLA_WIDTH + RWKV_WIDTH),
                  _const_spec((D_MODEL, 2 * D_MODEL)), _const_spec((1, 2 * D_MODEL)),
                  _const_spec((GLA_WIDTH, D_MODEL)), _const_spec((RWKV_WIDTH, D_MODEL)),
                  _const_spec((D_MODEL, D_MODEL)), _const_spec((1, D_MODEL)), _const_spec((1, D_MODEL)),
                  _const_spec((D_MODEL, D_FF)), _const_spec((1, D_FF)),
                  _const_spec((D_FF, D_MODEL)), _const_spec((1, D_MODEL)),
                  _const_spec((1, D_MODEL)), _const_spec((1, D_MODEL))],
        out_specs=mtok(D_MODEL),
        out_shape=jax.ShapeDtypeStruct((t, D_MODEL), F32),
        compiler_params=pltpu.CompilerParams(dimension_semantics=("arbitrary",),
                                             vmem_limit_bytes=VMEM_LIMIT_V7X),
        name="merge_mlp",
    )(x2, o_ab, p['w_merge'].astype(BF16), row(p['b_merge']),
      p['w_branch'][0].astype(BF16), p['w_branch'][1].astype(BF16), p['w_out'].astype(BF16),
      row(p['ln1_g']), row(p['ln1_b']), p['w_mlp_up'].astype(BF16), row(p['b_mlp_up']),
      p['w_mlp_down'].astype(BF16), row(p['b_mlp_down']), row(p['ln2_g']), row(p['ln2_b']))
    return out


def kernel(x, w_in, mu_shift, w_gk_up, b_gk, gla_norm_w, rwkv_w0, rwkv_w_up, rwkv_a0, rwkv_a_up,
           rwkv_g_up, rwkv_k_k, rwkv_k_a, rwkv_r_k, rwkv_gn_w, rwkv_gn_b, w_merge, b_merge, w_branch,
           w_out, ln1_g, ln1_b, w_mlp_up, b_mlp_up, w_mlp_down, b_mlp_down, ln2_g, ln2_b):
    bsz, seq, d = x.shape
    assert d == D_MODEL and seq % DENSE_SUB == 0
    tm_proj = PROJ_TILE if seq % PROJ_TILE == 0 else DENSE_SUB
    tm_mix = MIX_TILE if (bsz * seq) % MIX_TILE == 0 else DENSE_SUB
    params = dict(w_in=w_in, mu_shift=mu_shift, w_gk_up=w_gk_up, b_gk=b_gk, gla_norm_w=gla_norm_w,
                  rwkv_w0=rwkv_w0, rwkv_w_up=rwkv_w_up, rwkv_a0=rwkv_a0, rwkv_a_up=rwkv_a_up,
                  rwkv_g_up=rwkv_g_up, rwkv_k_k=rwkv_k_k, rwkv_k_a=rwkv_k_a, rwkv_r_k=rwkv_r_k,
                  rwkv_gn_w=rwkv_gn_w, rwkv_gn_b=rwkv_gn_b, w_merge=w_merge, b_merge=b_merge,
                  w_branch=w_branch, w_out=w_out, ln1_g=ln1_g, ln1_b=ln1_b, w_mlp_up=w_mlp_up,
                  b_mlp_up=b_mlp_up, w_mlp_down=w_mlp_down, b_mlp_down=b_mlp_down, ln2_g=ln2_g, ln2_b=ln2_b)
    x2 = x.reshape(bsz * seq, d)
    for l in range(w_in.shape[0]):
        x2 = _layer(x2, bsz, seq, {n: a[l] for n, a in params.items()}, tm_proj, tm_mix)
    return x2.reshape(bsz, seq, d)
```

```python
import functools

import jax
import jax.numpy as jnp
import numpy as np
from jax import lax
from jax.experimental import pallas as pl
from jax.experimental.pallas import tpu as pltpu

F32 = jnp.float32
BF16 = jnp.bfloat16

D_MODEL = 1024
D_FF = 4 * D_MODEL
DEPTH = 1
CHUNK = 64
SUB = 16
GLA_HEADS, GLA_DK, GLA_DV = 4, 64, 128
GLA_GATE_RANK = 16
GLA_GATE_NORMALIZER = 16.0
GLA_NORM_EPS = 1e-5
RWKV_HEADS, RWKV_HEAD = 8, 64
RWKV_GN_EPS = 64e-5
L2_EPS = 1e-12
LN_EPS = 1e-5
ALPHA = (2.0 * DEPTH) ** 0.25
GLA_QK = GLA_HEADS * GLA_DK
GLA_WIDTH = GLA_HEADS * GLA_DV
RWKV_WIDTH = RWKV_HEADS * RWKV_HEAD
LANES = 128
GLA_GK_OFF = 2 * GLA_QK + 2 * GLA_WIDTH
GLA_IN = GLA_GK_OFF + GLA_GATE_RANK
GLA_IN_PAD = GLA_GK_OFF + LANES
RWKV_W_LORA, RWKV_A_LORA, RWKV_G_LORA = 64, 64, 128
LORA_OFF = 3 * RWKV_WIDTH
LORA_IN = RWKV_W_LORA + RWKV_A_LORA + RWKV_G_LORA
RWKV_IN = LORA_OFF + LORA_IN
RWKV_PAIR = 2 * RWKV_HEAD
GLA_PACK = 2 * GLA_QK + 2 * GLA_WIDTH
RWKV_PACK = 7 * RWKV_WIDTH
FF_CHUNK = 1024
DENSE_SUB = 256
PROJ_TILE = 512
MIX_TILE = 512
MIXER_SEQS = 8
MIXER_SEQ_GROUP = 4
LOG2_E = float(np.log2(np.e))
GLA_SAFE_LOG2 = 96.0
VMEM_LIMIT_V7X = 56 * 1024 * 1024


def _dot(a, b):
    return jnp.dot(a, b, preferred_element_type=F32)


def _dot_nt(a, b):
    return lax.dot_general(a, b, (((1,), (1,)), ((), ())), preferred_element_type=F32)


def _dot_tn(a, b):
    return lax.dot_general(a, b, (((0,), (0,)), ((), ())), preferred_element_type=F32)


def _sigmoid(z):
    return 1.0 / (1.0 + jnp.exp(-z))


def _log_sigmoid(z):
    return jnp.minimum(z, 0.0) - jnp.log(1.0 + jnp.exp(-jnp.abs(z)))


def _split_bf16(a):
    hi = a.astype(BF16)
    lo = (a - hi.astype(F32)).astype(BF16)
    return hi, lo


def _const_spec(shape):
    nd = len(shape)
    return pl.BlockSpec(shape, lambda *_: (0,) * nd, pipeline_mode=pl.Buffered(1))


def _proj_kernel(x_ref, wg_ref, wr_ref, mu_ref, wgk_ref, bgk_ref, wlora_ref, w0_ref, a0_ref,
                 kk_ref, ka_ref, rk_ref, ones_ref,
                 ga_ref, gk_ref, rw_ref, lw_ref,
                 carry_ref, *, tiles_per_seq):
    def put(ref, rows, n, width, val):
        ref[rows, n * width:(n + 1) * width] = val.astype(BF16)

    i = pl.program_id(0)
    n_sub = x_ref.shape[0] // DENSE_SUB

    @pl.when(i % tiles_per_seq == 0)
    def _():
        carry_ref[...] = jnp.zeros_like(carry_ref)

    def project(j):
        xb = x_ref[j * DENSE_SUB:(j + 1) * DENSE_SUB, :].astype(BF16)
        return _dot(xb, wg_ref[...]), _dot(xb, wr_ref[...])

    def prepare(j, hg, hr, last_row):
        rows = slice(j * DENSE_SUB, (j + 1) * DENSE_SUB)
        put(ga_ref, rows, 0, GLA_QK, hg[:, 0:GLA_QK] * (GLA_DK ** -0.5))
        put(ga_ref, rows, 1, GLA_QK, hg[:, GLA_QK:2 * GLA_QK])
        put(ga_ref, rows, 1, GLA_WIDTH, hg[:, 2 * GLA_QK:2 * GLA_QK + GLA_WIDTH])
        g_out = hg[:, 2 * GLA_QK + GLA_WIDTH:2 * GLA_QK + 2 * GLA_WIDTH]
        put(ga_ref, rows, 2, GLA_WIDTH, g_out * _sigmoid(g_out))
        z = _dot(hg[:, GLA_GK_OFF:GLA_IN_PAD].astype(BF16), wgk_ref[...]) + bgk_ref[...]
        gk_ref[rows, :] = _log_sigmoid(z) * (1.0 / GLA_GATE_NORMALIZER)

        row = lax.broadcasted_iota(jnp.int32, (DENSE_SUB, 1), 0)
        prev = jnp.where(row == 0, last_row, pltpu.roll(hr, 1, 0))
        u = hr + (prev - hr) * mu_ref[...]
        r = u[:, 0:RWKV_WIDTH]
        k = u[:, RWKV_WIDTH:2 * RWKV_WIDTH]
        v = u[:, 2 * RWKV_WIDTH:3 * RWKV_WIDTH]
        low = u[:, LORA_OFF:LORA_OFF + LORA_IN]
        col = lax.broadcasted_iota(jnp.int32, (1, LORA_IN), 1)
        low = jnp.where(col < RWKV_W_LORA, jnp.tanh(low),
                        jnp.where(col < RWKV_W_LORA + RWKV_A_LORA, low, _sigmoid(low)))
        lo = _dot(low.astype(BF16), wlora_ref[...])
        lw_ref[rows, :] = -float(np.exp(-0.5)) * _sigmoid(w0_ref[...] + lo[:, 0:RWKV_WIDTH])
        a = _sigmoid(a0_ref[...] + lo[:, RWKV_WIDTH:2 * RWKV_WIDTH])
        put(rw_ref, rows, 5, RWKV_WIDTH, lo[:, 2 * RWKV_WIDTH:3 * RWKV_WIDTH])

        ones = ones_ref[...]
        kk = k * kk_ref[...]
        ss = _dot((kk * kk).astype(BF16), ones)
        kkn = kk / jnp.maximum(jnp.sqrt(ss), L2_EPS)
        k2 = k * (1.0 + (a - 1.0) * ka_ref[...])
        bonus = _dot((r * k2 * rk_ref[...]).astype(BF16), ones) * v
        put(rw_ref, rows, 0, RWKV_WIDTH, r)
        put(rw_ref, rows, 1, RWKV_WIDTH, k2)
        put(rw_ref, rows, 2, RWKV_WIDTH, v)
        put(rw_ref, rows, 3, RWKV_WIDTH, kkn)
        put(rw_ref, rows, 4, RWKV_WIDTH, kkn * a)
        put(rw_ref, rows, 6, RWKV_WIDTH, bonus)
        return hr[DENSE_SUB - 1:DENSE_SUB, :]

    last_row = carry_ref[...]
    pending = project(0)
    for j in range(n_sub):
        following = project(j + 1) if j + 1 < n_sub else None
        last_row = prepare(j, *pending, last_row)
        pending = following
    carry_ref[...] = last_row


def _gla_part(q_ref, k_ref, v_ref, sg_ref, gk_ref, nw_ref, tri_ref, ebig_ref, o_ref,
              s_ref, kp_ref, bp_ref, *, n_seq, seq_group):
    blk = n_seq * CHUNK
    heads = range(GLA_HEADS)
    subs = range(n_seq)
    hsl = [slice(h * GLA_DK, (h + 1) * GLA_DK) for h in heads]
    vsl = [slice(h * GLA_DV, (h + 1) * GLA_DV) for h in heads]
    rows = [slice(g * CHUNK, (g + 1) * CHUNK) for g in subs]
    n_pair = GLA_HEADS // 2
    ksl = [slice(p * 2 * GLA_DK, (p + 1) * 2 * GLA_DK) for p in range(n_pair)]
    wsl = [slice(p * 2 * GLA_DV, (p + 1) * 2 * GLA_DV) for p in range(n_pair)]
    chains = [(g, p) for g in subs for p in range(n_pair)]
    lane_k = lax.broadcasted_iota(jnp.int32, (1, GLA_QK), 1)
    lane_v = lax.broadcasted_iota(jnp.int32, (1, GLA_WIDTH), 1)
    first_k = (lane_k & GLA_DK) == 0
    first_v = (lane_v & GLA_DV) == 0
    first_s = lax.broadcasted_iota(jnp.int32, (GLA_DV, 2 * GLA_DK), 1) < GLA_DK
    lane_c = lax.broadcasted_iota(jnp.int32, (CHUNK, 2 * GLA_DK), 1)
    causal = lax.broadcasted_iota(jnp.int32, (CHUNK, 2 * GLA_DK), 0) >= (lane_c & (GLA_DK - 1))
    row_c = lax.broadcasted_iota(jnp.int32, (CHUNK, 1), 0)

    def load(ref):
        return ref[...].reshape(blk, ref.shape[-1])

    gk = load(gk_ref)
    n_rows = seq_group * CHUNK
    b = jnp.concatenate([_dot(tri_ref[...], jnp.concatenate(_split_bf16(gk[j * n_rows:(j + 1) * n_rows, :]), axis=0))
                         for j in range(n_seq // seq_group)], axis=0) * LOG2_E

    def finish(k, v_all, qe, score):
        b_last = {g: b[(g + 1) * CHUNK - 1:(g + 1) * CHUNK, :] for g in subs}
        kdec = {g: (k[rows[g], :] * jnp.exp2(b_last[g] - b[rows[g], :])).astype(BF16) for g in subs}
        kv = {(g, p): _dot_tn(v_all[rows[g], wsl[p]], kdec[g][:, ksl[p]]) for g, p in chains}
        state = {(g, p): s_ref[g, :, ksl[p]] for g, p in chains}
        entering = {ch: jnp.concatenate([jnp.where(first_s, state[ch], 0.0),
                                         jnp.where(first_s, 0.0, state[ch])], axis=0).astype(BF16) for ch in chains}
        new_state = {(g, p): state[g, p] * jnp.exp2(b_last[g])[:, ksl[p]]
                     + jnp.where(first_s, kv[g, p][0:GLA_DV], kv[g, p][GLA_DV:]) for g, p in chains}
        inter = {(g, p): _dot_nt(qe[rows[g], ksl[p]], entering[g, p]) for g, p in chains}
        sg = load(sg_ref)
        outs = []
        for g in subs:
            pieces = []
            for p in range(n_pair):
                op = inter[g, p] + score[g, p]
                for oh in (op[:, 0:GLA_DV], op[:, GLA_DV:]):
                    ms = jnp.mean(oh * oh, axis=-1, keepdims=True)
                    pieces.append(oh * lax.rsqrt(ms + GLA_NORM_EPS))
            o = jnp.concatenate(pieces, axis=-1) * nw_ref[...]
            outs.append((o * sg[rows[g], :].astype(F32)).astype(BF16))
        o_ref[...] = jnp.stack(outs, axis=0)
        return new_state

    safe = jnp.min(b) >= -GLA_SAFE_LOG2

    q = load(q_ref).astype(F32)
    k = load(k_ref).astype(F32)
    v_all = load(v_ref)
    qe = (q * jnp.exp2(b)).astype(BF16)
    kn = (k * jnp.exp2(-b)).astype(BF16)
    kn_1, kn_2 = kn * jnp.where(first_k, 1.0, 0.0).astype(BF16), kn * jnp.where(first_k, 0.0, 1.0).astype(BF16)
    v_1, v_2 = v_all * jnp.where(first_v, 1.0, 0.0).astype(BF16), v_all * jnp.where(first_v, 0.0, 1.0).astype(BF16)
    raw = {(g, p): _dot_nt(qe[rows[g], ksl[p]],
                           jnp.concatenate([kn_1[rows[g], ksl[p]], kn_2[rows[g], ksl[p]]], axis=0))
           for g, p in chains}
    sc_fast = {ch: jnp.where(causal, raw[ch], 0.0).astype(BF16) for ch in chains}
    state_fast = finish(k, v_all, qe, {
        (g, p): _dot(sc_fast[g, p], jnp.concatenate([v_1[rows[g], wsl[p]], v_2[rows[g], wsl[p]]], axis=0))
        for g, p in chains})

    @pl.when(safe)
    def _():
        for g, p in chains:
            s_ref[g, :, ksl[p]] = state_fast[g, p]

    @pl.when(jnp.logical_not(safe))
    def _():
        kp_ref[0:SUB, :] = jnp.zeros((SUB, GLA_QK), F32)
        bp_ref[0:SUB, :] = jnp.zeros((SUB, GLA_QK), F32)
        kp_ref[SUB:SUB + blk, :] = k
        bp_ref[SUB:SUB + blk, :] = b
        rmod = lax.broadcasted_iota(jnp.int32, (blk, 1), 0) % SUB
        xs = [(q * k).astype(BF16)]
        for d in range(1, SUB):
            ks = kp_ref[SUB - d:SUB - d + blk, :]
            bs = bp_ref[SUB - d:SUB - d + blk, :]
            xs.append(jnp.where(rmod >= d, q * ks * jnp.exp2(b - bs), 0.0).astype(BF16))
        s_diag = _dot(jnp.concatenate(xs, axis=1), ebig_ref[...])
        lane = lax.broadcasted_iota(jnp.int32, (1, LANES), 1)
        score = {}
        for g in subs:
            bg = b[rows[g], :]
            for h in heads:
                qh, kh, bh = q[rows[g], hsl[h]], k[rows[g], hsl[h]], bg[:, hsl[h]]
                md = jnp.where((lane >= SUB * h) & (lane < SUB * (h + 1)), s_diag[rows[g], :], 0.0)
                sc_rows = [jnp.zeros((SUB, CHUNK), F32)]
                for sb in range(1, CHUNK // SUB):
                    lo_r = sb * SUB
                    ref = bh[lo_r - 1:lo_r, :]
                    qi = qh[lo_r:lo_r + SUB, :] * jnp.exp2(bh[lo_r:lo_r + SUB, :] - ref)
                    ki = jnp.where(row_c < lo_r, kh * jnp.exp2(jnp.minimum(ref - bh, 0.0)), 0.0)
                    sc_rows.append(_dot_nt(qi.astype(BF16), ki.astype(BF16)))
                sc = jnp.concatenate(sc_rows, axis=0) + pltpu.roll(
                    md, LANES - (SUB - 1) - SUB * h, 1, stride=1, stride_axis=0)[:, 0:CHUNK]
                score[g, h] = _dot(sc.astype(BF16), v_all[rows[g], vsl[h]])
        state_slow = finish(k, v_all, qe, {(g, p): jnp.concatenate([score[g, 2 * p], score[g, 2 * p + 1]], axis=1)
                                          for g, p in chains})
        for g, p in chains:
            s_ref[g, :, ksl[p]] = state_slow[g, p]


def _rwkv_part(r_ref, k_ref, v_ref, kk_ref, kb_ref, lw_ref, g_ref, bonus_ref, gnw_ref, gnb_ref,
               tri_ref, ones_ref, o_ref, s_ref, y_ref, *, n_seq, seq_group):
    n_pair = RWKV_HEADS // 2
    psl = [slice(p * RWKV_PAIR, (p + 1) * RWKV_PAIR) for p in range(n_pair)]
    subs = range(seq_group)
    rows = [slice(g * CHUNK, (g + 1) * CHUNK) for g in subs]
    chains = [(g, p) for g in subs for p in range(n_pair)]
    n_rows = seq_group * CHUNK

    lane_w = lax.broadcasted_iota(jnp.int32, (1, RWKV_WIDTH), 1)
    first_w = (lane_w & RWKV_HEAD) == 0

    keep_1 = jnp.where(first_w, 1.0, 0.0).astype(BF16)
    keep_2 = jnp.where(first_w, 0.0, 1.0).astype(BF16)

    def halves(a):
        return a * keep_1, a * keep_2

    lane_p = lax.broadcasted_iota(jnp.int32, (CHUNK, RWKV_PAIR), 1)
    row_p = lax.broadcasted_iota(jnp.int32, (CHUNK, RWKV_PAIR), 0)
    first_p = lane_p < RWKV_HEAD
    col_p = lane_p & (RWKV_HEAD - 1)
    strict = row_p > col_p
    incl = row_p >= col_p
    eye = (row_p == col_p).astype(F32)

    def bdiag(a):
        return jnp.concatenate([jnp.where(first_p, a, 0.0), jnp.where(first_p, 0.0, a)], axis=0).astype(BF16)

    def cut(a, ch):
        return a[rows[ch[0]], psl[ch[1]]]

    def prepare(grp):
        def load(ref):
            return ref[grp * seq_group:(grp + 1) * seq_group].reshape(n_rows, RWKV_WIDTH)

        lw = load(lw_ref)
        c = _dot(tri_ref[...], jnp.concatenate(_split_bf16(lw), axis=0))
        e_nc = jnp.exp(-c)
        bt_f = load(kb_ref).astype(F32) * e_nc
        kt_f = load(k_ref).astype(F32) * e_nc
        v_all = load(v_ref)
        at_all = (-load(kk_ref).astype(F32) * jnp.exp(c - lw)).astype(BF16)
        d = dict(v_all=v_all, at_all=at_all, rt_all=(load(r_ref).astype(F32) * jnp.exp(c)).astype(BF16))
        d['at_1'], d['at_2'] = halves(at_all)
        d['bt_1'], d['bt_2'] = halves(bt_f.astype(BF16))
        d['kt_1'], d['kt_2'] = halves(kt_f.astype(BF16))
        d['v_1'], d['v_2'] = halves(v_all)
        d['e_last'] = {g: jnp.exp(c[(g + 1) * CHUNK - 1:(g + 1) * CHUNK, :]) for g in subs}
        d['bh'] = {g: (bt_f[rows[g], :] * d['e_last'][g]).astype(BF16) for g in subs}
        d['kh'] = {g: (kt_f[rows[g], :] * d['e_last'][g]).astype(BF16) for g in subs}
        return d

    def first_level(d):
        return {ch: _dot_nt(jnp.concatenate([cut(d['at_all'], ch), cut(d['rt_all'], ch)], axis=0),
                            jnp.concatenate([cut(d['bt_1'], ch), cut(d['bt_2'], ch),
                                             cut(d['kt_1'], ch), cut(d['kt_2'], ch)], axis=0))
                for ch in chains}

    def run_chains(grp, d, a_mat):
        a_ab = {ch: jnp.where(strict, a_mat[ch][0:CHUNK, 0:RWKV_PAIR], 0.0) for ch in chains}
        a_kr = {ch: jnp.concatenate([jnp.where(strict, a_mat[ch][0:CHUNK, RWKV_PAIR:], 0.0),
                                     jnp.where(incl, a_mat[ch][CHUNK:, RWKV_PAIR:], 0.0)], axis=0).astype(BF16)
                for ch in chains}
        a_rb = {ch: jnp.where(incl, a_mat[ch][CHUNK:, 0:RWKV_PAIR], 0.0).astype(BF16) for ch in chains}
        av = {ch: _dot(a_kr[ch], jnp.concatenate([cut(d['v_1'], ch), cut(d['v_2'], ch)], axis=0)) for ch in chains}
        t_inv = {ch: eye + a_ab[ch] for ch in chains}
        pw = {ch: _dot(a_ab[ch].astype(BF16), bdiag(a_ab[ch])) for ch in chains}
        for _ in range(4):
            pt = {ch: _dot(jnp.concatenate([pw[ch], t_inv[ch]], axis=0).astype(BF16), bdiag(pw[ch]))
                  for ch in chains}
            pw = {ch: pt[ch][0:CHUNK] for ch in chains}
            t_inv = {ch: t_inv[ch] + pt[ch][CHUNK:] for ch in chains}
        t_inv = {ch: t_inv[ch] + _dot(t_inv[ch].astype(BF16), bdiag(pw[ch])) for ch in chains}
        wu = {ch: _dot(t_inv[ch].astype(BF16),
                       jnp.concatenate([jnp.concatenate([cut(d['at_1'], ch), cut(d['at_2'], ch)], axis=0),
                                        bdiag(av[ch][0:CHUNK])], axis=1)) for ch in chains}
        wr_til = {ch: jnp.concatenate([wu[ch][:, 0:RWKV_PAIR].astype(BF16), cut(d['rt_all'], ch)], axis=0)
                  for ch in chains}
        state = {(g, p): s_ref[grp * seq_group + g, :, psl[p]] for g, p in chains}
        ws = {ch: _dot_nt(wr_til[ch], bdiag(state[ch])) for ch in chains}
        u = {ch: ws[ch][0:CHUNK] + wu[ch][:, RWKV_PAIR:] for ch in chains}
        for g, p in chains:
            y_ref[grp * n_rows + g * CHUNK:grp * n_rows + (g + 1) * CHUNK, psl[p]] = (
                ws[g, p][CHUNK:] + _dot(a_rb[g, p], bdiag(u[g, p])) + av[g, p][CHUNK:])
        upd = {(g, p): _dot_tn(jnp.concatenate([u[g, p].astype(BF16), d['v_all'][rows[g], psl[p]]], axis=0),
                               jnp.concatenate([d['bh'][g][:, psl[p]], d['kh'][g][:, psl[p]]], axis=0))
               for g, p in chains}
        for g, p in chains:
            s_ref[grp * seq_group + g, :, psl[p]] = (state[g, p] * d['e_last'][g][:, psl[p]]
                                                     + jnp.where(first_p, upd[g, p][0:CHUNK], upd[g, p][CHUNK:]))

    def normalise(grp):
        ones = ones_ref[...]
        y = y_ref[grp * n_rows:(grp + 1) * n_rows, :]
        mu = _dot(y.astype(BF16), ones) * (1.0 / RWKV_HEAD)
        yc = y - mu
        var = _dot((yc * yc).astype(BF16), ones) * (1.0 / RWKV_HEAD)
        yn = yc * lax.rsqrt(var + RWKV_GN_EPS) * gnw_ref[...] + gnb_ref[...]
        gs = slice(grp * seq_group, (grp + 1) * seq_group)
        o = ((yn + bonus_ref[gs].reshape(n_rows, RWKV_WIDTH).astype(F32))
             * g_ref[gs].reshape(n_rows, RWKV_WIDTH).astype(F32)).astype(BF16)
        o_ref[gs] = o.reshape(seq_group, CHUNK, RWKV_WIDTH)

    n_grp = n_seq // seq_group
    pending = prepare(0)
    for grp in range(n_grp):
        following = prepare(grp + 1) if grp + 1 < n_grp else None
        run_chains(grp, pending, first_level(pending))
        if grp > 0:
            normalise(grp - 1)
        pending = following
    normalise(n_grp - 1)


def _mixers_kernel(ga_ref, ggk_ref, gnw_ref, ebig_ref, rw_ref, lw_ref, rnw_ref, rnb_ref,
                   tri_ref, ones_ref, oa_ref, ob_ref,
                   gs_ref, kp_ref, bp_ref, rs_ref, y_ref, *, n_seq, seq_group):
    gq_ref, gk_ref = (ga_ref.at[:, :, n * GLA_QK:(n + 1) * GLA_QK] for n in range(2))
    gv_ref, gsg_ref = (ga_ref.at[:, :, n * GLA_WIDTH:(n + 1) * GLA_WIDTH] for n in range(1, 3))
    r_ref, k_ref, v_ref, kk_ref, kb_ref, g_ref, bonus_ref = (
        rw_ref.at[:, :, n * RWKV_WIDTH:(n + 1) * RWKV_WIDTH] for n in range(7))
    @pl.when(pl.program_id(1) == 0)
    def _():
        gs_ref[...] = jnp.zeros_like(gs_ref)
        rs_ref[...] = jnp.zeros_like(rs_ref)

    _rwkv_part(r_ref, k_ref, v_ref, kk_ref, kb_ref, lw_ref, g_ref, bonus_ref, rnw_ref, rnb_ref,
               tri_ref, ones_ref, ob_ref, rs_ref, y_ref, n_seq=n_seq, seq_group=seq_group)
    _gla_part(gq_ref, gk_ref, gv_ref, gsg_ref, ggk_ref, gnw_ref, tri_ref, ebig_ref, oa_ref,
              gs_ref, kp_ref, bp_ref, n_seq=n_seq, seq_group=seq_group)


def _layernorm(h, g, b):
    mu = jnp.mean(h, axis=-1, keepdims=True)
    hc = h - mu
    var = jnp.mean(hc * hc, axis=-1, keepdims=True)
    return hc * lax.rsqrt(var + LN_EPS) * g + b


def _mix_kernel(x_ref, oa_ref, ob_ref, wm_ref, bm_ref, wba_ref, wbb_ref, wout_ref, ln1g_ref, ln1b_ref,
                w1_ref, b1_ref, w2_ref, b2_ref, ln2g_ref, ln2b_ref, out_ref):
    n_sub = x_ref.shape[0] // DENSE_SUB

    def merge(j):
        rows = slice(j * DENSE_SUB, (j + 1) * DENSE_SUB)
        x = x_ref[rows, :]
        xb = x.astype(BF16)
        ya = _dot(oa_ref[rows, :], wba_ref[...])
        yb = _dot(ob_ref[rows, :], wbb_ref[...])
        ga = _sigmoid(_dot(xb, wm_ref[:, 0:D_MODEL]) + bm_ref[:, 0:D_MODEL])
        m = ga * ya
        gb = _sigmoid(_dot(xb, wm_ref[:, D_MODEL:]) + bm_ref[:, D_MODEL:])
        m = m + gb * yb
        return ALPHA * x + _dot(m.astype(BF16), wout_ref[...])

    def mlp(j, z):
        rows = slice(j * DENSE_SUB, (j + 1) * DENSE_SUB)
        x1 = _layernorm(z, ln1g_ref[...], ln1b_ref[...])
        x1b = x1.astype(BF16)
        acc = jnp.zeros_like(x1)
        for c in range(D_FF // FF_CHUNK):
            cs = slice(c * FF_CHUNK, (c + 1) * FF_CHUNK)
            h = jnp.maximum(_dot(x1b, w1_ref[:, cs]) + b1_ref[:, cs], 0.0)
            acc = acc + _dot((h * h).astype(BF16), w2_ref[cs, :])
        out_ref[rows, :] = _layernorm(ALPHA * x1 + acc + b2_ref[...], ln2g_ref[...], ln2b_ref[...])

    pending = merge(0)
    for j in range(n_sub):
        following = merge(j + 1) if j + 1 < n_sub else None
        mlp(j, pending)
        pending = following


def _block_ones(n, blk):
    idx = np.arange(n) // blk
    return jnp.asarray(idx[:, None] == idx[None, :], BF16)


def _layer(x2, bsz, seq, p, tm_proj, tm_mix):
    t = bsz * seq
    row = lambda a: a.reshape(1, -1).astype(F32)

    w_in = p['w_in']
    wg = jnp.pad(w_in[:, :GLA_IN], ((0, 0), (0, GLA_IN_PAD - GLA_IN))).astype(BF16)
    wr = w_in[:, GLA_IN:].astype(BF16)
    wgk = jnp.pad(p['w_gk_up'], ((0, LANES - GLA_GATE_RANK), (0, 0))).astype(BF16)
    a_off = RWKV_W_LORA + RWKV_A_LORA
    wlora = jnp.zeros((LORA_IN, 3 * RWKV_WIDTH), F32)
    wlora = wlora.at[0:RWKV_W_LORA, 0:RWKV_WIDTH].set(p['rwkv_w_up'])
    wlora = wlora.at[RWKV_W_LORA:a_off, RWKV_WIDTH:2 * RWKV_WIDTH].set(p['rwkv_a_up'])
    wlora = wlora.at[a_off:LORA_IN, 2 * RWKV_WIDTH:].set(p['rwkv_g_up']).astype(BF16)
    ones_r = _block_ones(RWKV_WIDTH, RWKV_HEAD)

    n_tiles = t // tm_proj
    tok = lambda w: pl.BlockSpec((tm_proj, w), lambda i: (i, 0))
    bshape = lambda w: jax.ShapeDtypeStruct((t, w), BF16)
    proj_out = pl.pallas_call(
        functools.partial(_proj_kernel, tiles_per_seq=seq // tm_proj),
        grid=(n_tiles,),
        in_specs=[tok(D_MODEL), _const_spec((D_MODEL, GLA_IN_PAD)), _const_spec((D_MODEL, RWKV_IN)),
                  _const_spec((1, RWKV_IN)), _const_spec((LANES, GLA_QK)), _const_spec((1, GLA_QK)),
                  _const_spec((LORA_IN, 3 * RWKV_WIDTH))] + [_const_spec((1, RWKV_WIDTH))] * 5
                 + [_const_spec((RWKV_WIDTH, RWKV_WIDTH))],
        out_specs=[tok(GLA_PACK), tok(GLA_QK), tok(RWKV_PACK), tok(RWKV_WIDTH)],
        out_shape=[bshape(GLA_PACK), jax.ShapeDtypeStruct((t, GLA_QK), F32),
                   bshape(RWKV_PACK), jax.ShapeDtypeStruct((t, RWKV_WIDTH), F32)],
        scratch_shapes=[pltpu.VMEM((1, RWKV_IN), F32)],
        compiler_params=pltpu.CompilerParams(dimension_semantics=("arbitrary",),
                                             vmem_limit_bytes=VMEM_LIMIT_V7X),
        name="proj_prep",
    )(x2, wg, wr, row(p['mu_shift']), wgk, row(p['b_gk']), wlora, row(p['rwkv_w0']), row(p['rwkv_a0']),
      row(p['rwkv_k_k']), row(p['rwkv_k_a']), row(p['rwkv_r_k']), ones_r)
    g_pack, ggk, r_pack, rlw = proj_out

    def block_tri2(n_sub):
        tri = np.kron(np.eye(n_sub), np.tril(np.ones((CHUNK, CHUNK))))
        return jnp.asarray(np.concatenate([tri, tri], axis=1), BF16)

    rid = np.arange(SUB * GLA_QK)
    cid = SUB * ((rid % GLA_QK) // GLA_DK) + (SUB - 1) - rid // GLA_QK
    e_big = jnp.asarray(cid[:, None] == np.arange(LANES)[None, :], BF16)

    n_seq = max(n for n in (1, 2, 4, MIXER_SEQS) if bsz % n == 0)
    seq_group = min(MIXER_SEQ_GROUP, n_seq)
    sspec = lambda w: pl.BlockSpec((n_seq, CHUNK, w), lambda b, n: (b, n, 0))
    seq3 = lambda a: a.reshape(bsz, seq, a.shape[-1])
    o_a, o_b = pl.pallas_call(
        functools.partial(_mixers_kernel, n_seq=n_seq, seq_group=seq_group),
        grid=(bsz // n_seq, seq // CHUNK),
        in_specs=[sspec(GLA_PACK), sspec(GLA_QK), _const_spec((1, GLA_WIDTH)), _const_spec((SUB * GLA_QK, LANES)),
                  sspec(RWKV_PACK), sspec(RWKV_WIDTH)] + [_const_spec((1, RWKV_WIDTH))] * 2
                 + [_const_spec((seq_group * CHUNK, 2 * seq_group * CHUNK)), _const_spec((RWKV_WIDTH, RWKV_WIDTH))],
        out_specs=[sspec(GLA_WIDTH), sspec(RWKV_WIDTH)],
        out_shape=[jax.ShapeDtypeStruct((bsz, seq, GLA_WIDTH), BF16),
                   jax.ShapeDtypeStruct((bsz, seq, RWKV_WIDTH), BF16)],
        scratch_shapes=[pltpu.VMEM((n_seq, GLA_DV, GLA_QK), F32),
                        pltpu.VMEM((SUB + n_seq * CHUNK, GLA_QK), F32),
                        pltpu.VMEM((SUB + n_seq * CHUNK, GLA_QK), F32),
                        pltpu.VMEM((n_seq, RWKV_HEAD, RWKV_WIDTH), F32),
                        pltpu.VMEM((n_seq * CHUNK, RWKV_WIDTH), F32)],
        compiler_params=pltpu.CompilerParams(dimension_semantics=("arbitrary", "arbitrary"),
                                             vmem_limit_bytes=VMEM_LIMIT_V7X),
        name="token_mixers",
    )(seq3(g_pack), seq3(ggk), row(jnp.tile(p['gla_norm_w'], GLA_HEADS)), e_big, seq3(r_pack), seq3(rlw),
      row(p['rwkv_gn_w']), row(p['rwkv_gn_b']), block_tri2(seq_group), ones_r)
    o_a = o_a.reshape(t, GLA_WIDTH)
    o_b = o_b.reshape(t, RWKV_WIDTH)

    mtok = lambda w: pl.BlockSpec((tm_mix, w), lambda i: (i, 0))
    out = pl.pallas_call(
        _mix_kernel,
        grid=(t // tm_mix,),
        in_specs=[mtok(D_MODEL), mtok(GLA_WIDTH), mtok(RWKV_WIDTH),
                  _const_spec((D_MODEL, 2 * D_MODEL)), _const_spec((1, 2 * D_MODEL)),
                  _const_spec((GLA_WIDTH, D_MODEL)), _const_spec((RWKV_WIDTH, D_MODEL)),
                  _const_spec((D_MODEL, D_MODEL)), _const_spec((1, D_MODEL)), _const_spec((1, D_MODEL)),
                  _const_spec((D_MODEL, D_FF)), _const_spec((1, D_FF)),
                  _const_spec((D_FF, D_MODEL)), _const_spec((1, D_MODEL)),
                  _const_spec((1, D_MODEL)), _const_spec((1, D_MODEL))],
        out_specs=mtok(D_MODEL),
        out_shape=jax.ShapeDtypeStruct((t, D_MODEL), F32),
        compiler_params=pltpu.CompilerParams(dimension_semantics=("arbitrary",),
                                             vmem_limit_bytes=VMEM_LIMIT_V7X),
        name="merge_mlp",
    )(x2, o_a, o_b, p['w_merge'].astype(BF16), row(p['b_merge']),
      p['w_branch'][0].astype(BF16), p['w_branch'][1].astype(BF16), p['w_out'].astype(BF16),
      row(p['ln1_g']), row(p['ln1_b']), p['w_mlp_up'].astype(BF16), row(p['b_mlp_up']),
      p['w_mlp_down'].astype(BF16), row(p['b_mlp_down']), row(p['ln2_g']), row(p['ln2_b']))
    return out


def kernel(x, w_in, mu_shift, w_gk_up, b_gk, gla_norm_w, rwkv_w0, rwkv_w_up, rwkv_a0, rwkv_a_up,
           rwkv_g_up, rwkv_k_k, rwkv_k_a, rwkv_r_k, rwkv_gn_w, rwkv_gn_b, w_merge, b_merge, w_branch,
           w_out, ln1_g, ln1_b, w_mlp_up, b_mlp_up, w_mlp_down, b_mlp_down, ln2_g, ln2_b):
    bsz, seq, d = x.shape
    assert d == D_MODEL and seq % DENSE_SUB == 0
    tm_proj = PROJ_TILE if seq % PROJ_TILE == 0 else DENSE_SUB
    tm_mix = MIX_TILE if (bsz * seq) % MIX_TILE == 0 else DENSE_SUB
    params = dict(w_in=w_in, mu_shift=mu_shift, w_gk_up=w_gk_up, b_gk=b_gk, gla_norm_w=gla_norm_w,
                  rwkv_w0=rwkv_w0, rwkv_w_up=rwkv_w_up, rwkv_a0=rwkv_a0, rwkv_a_up=rwkv_a_up,
                  rwkv_g_up=rwkv_g_up, rwkv_k_k=rwkv_k_k, rwkv_k_a=rwkv_k_a, rwkv_r_k=rwkv_r_k,
                  rwkv_gn_w=rwkv_gn_w, rwkv_gn_b=rwkv_gn_b, w_merge=w_merge, b_merge=b_merge,
                  w_branch=w_branch, w_out=w_out, ln1_g=ln1_g, ln1_b=ln1_b, w_mlp_up=w_mlp_up,
                  b_mlp_up=b_mlp_up, w_mlp_down=w_mlp_down, b_mlp_down=b_mlp_down, ln2_g=ln2_g, ln2_b=ln2_b)
    x2 = x.reshape(bsz * seq, d)
    for l in range(w_in.shape[0]):
        x2 = _layer(x2, bsz, seq, {n: a[l] for n, a in params.items()}, tm_proj, tm_mix)
    return x2.reshape(bsz, seq, d)
```
